```python
import math
import jax
import jax.numpy as jnp
from jax import lax
import numpy as np

D_MODEL = 4096
BATCH = 4
SEQ = 4096
DEPTH = 1


HEAD_DIM = 128
HGRN_DK = 128
HGRN_DV = 128
N_HGRN_HEADS = (D_MODEL // 2) // HGRN_DV
HGRN_F_DIM = N_HGRN_HEADS * HGRN_DK
HGRN_WIDTH = N_HGRN_HEADS * HGRN_DV
N_MOBA_HEADS = (D_MODEL // 2) // HEAD_DIM
MOBA_WIDTH = N_MOBA_HEADS * HEAD_DIM
D_MIX = HGRN_WIDTH + MOBA_WIDTH
D_IN_PROJ = 2 * HGRN_F_DIM + 2 * HGRN_WIDTH + 3 * MOBA_WIDTH
D_FF = ((8 * D_MODEL // 3 + 255) // 256) * 256
MOBA_BLOCK = 256
MOBA_TOPK = 3
HGRN_CHUNK = 64
Q_CHUNK = 16
MACARON_WEIGHT = 0.5
RMS_EPS = 1e-6

kernel_name = 'hybrid_hgrn2_moba_macaron_layer'


def rmsnorm(x, g):
    xf = x.astype(jnp.float32)
    r = lax.rsqrt(jnp.mean(xf * xf, axis=-1, keepdims=True) + RMS_EPS)
    return (xf * r).astype(x.dtype) * g


def swiglu(h, w_gate, w_up, w_down):
    return (jax.nn.silu(h @ w_gate) * (h @ w_up)) @ w_down


def _heads(a, n_heads):
    b_, s_, w_ = a.shape
    return a.reshape(b_, s_, n_heads, w_ // n_heads).transpose(0, 2, 1, 3)


def _merge(a):
    b_, h_, s_, d_ = a.shape
    return a.transpose(0, 2, 1, 3).reshape(b_, s_, h_ * d_)


def hgrn2_chunkwise(q, k, v, log_f):
    b_, h_, s_, dk = q.shape
    dv = v.shape[-1]
    n_chunks = s_ // HGRN_CHUNK

    def to_chunks(a):
        return jnp.moveaxis(a.reshape(b_, h_, n_chunks, HGRN_CHUNK, a.shape[-1]), 2, 0)

    causal = jnp.tril(jnp.ones((HGRN_CHUNK, HGRN_CHUNK), dtype=bool))

    def step(state, inp):
        qc, kc, vc, gc = inp
        cum = jnp.cumsum(gc, axis=2)
        o_inter = jnp.einsum('bhtk,bhkv->bhtv', qc * jnp.exp(cum), state)
        rel = cum[:, :, :, None, :] - cum[:, :, None, :, :]
        decay = jnp.exp(jnp.where(causal[:, :, None], rel, -jnp.inf))
        scores = jnp.einsum('bhtk,bhsk,bhtsk->bhts', qc, kc, decay)
        o_intra = jnp.einsum('bhts,bhsv->bhtv', scores, vc)
        last = cum[:, :, -1, :]
        k_to_end = kc * jnp.exp(last[:, :, None, :] - cum)
        new_state = jnp.exp(last)[..., None] * state + jnp.einsum('bhsk,bhsv->bhkv', k_to_end, vc)
        return new_state, o_inter + o_intra

    state0 = jnp.zeros((b_, h_, dk, dv), jnp.float32)
    _, o = lax.scan(step, state0, (to_chunks(q), to_chunks(k), to_chunks(v), to_chunks(log_f)))
    return jnp.moveaxis(o, 0, 2).reshape(b_, h_, s_, dv)


def hgrn2_mixer(hq, hf, hi, hg, lower_bound, norm_gain):
    f32 = jnp.float32
    forget = lower_bound + (1.0 - lower_bound) * jax.nn.sigmoid(hf.astype(f32))
    q = _heads(jax.nn.silu(hq.astype(f32)) * HGRN_DK ** -0.5, N_HGRN_HEADS)
    k = _heads(1.0 - forget, N_HGRN_HEADS)
    log_f = _heads(jnp.log(forget), N_HGRN_HEADS)
    v = _heads(hi.astype(f32), N_HGRN_HEADS)
    o = hgrn2_chunkwise(q, k, v, log_f)
    o = rmsnorm(o, norm_gain)
    o = _merge(o) * jax.nn.silu(hg.astype(f32))
    return o.astype(hq.dtype)


def moba_attention(q, k, v):
    f32 = jnp.float32
    b_, h_, s_, d_ = q.shape
    L = MOBA_BLOCK
    nb = -(-s_ // L)
    s_pad = nb * L
    topk = min(MOBA_TOPK, max(nb - 1, 1))
    scale = d_ ** -0.5
    slopes = jnp.exp2(-8.0 * jnp.arange(1, h_ + 1, dtype=f32) / h_)
    pad = ((0, 0), (0, 0), (0, s_pad - s_), (0, 0))
    kb = jnp.pad(k, pad).reshape(b_, h_, nb, L, d_)
    vb = jnp.pad(v, pad).reshape(b_, h_, nb, L, d_)
    k_mean = jnp.mean(kb.astype(f32), axis=3)
    gate = jnp.einsum('bhtd,bhnd->bhtn', q.astype(f32), k_mean)
    q_block = jnp.arange(s_) // L
    past = jnp.arange(nb)[None, :] < q_block[:, None]
    gate = jnp.where(past, gate, -jnp.inf)
    _, sel = lax.top_k(gate, topk)

    n_qc = s_ // Q_CHUNK
    qc_all = jnp.moveaxis(q.reshape(b_, h_, n_qc, Q_CHUNK, d_), 2, 0)
    sel_all = jnp.moveaxis(sel.reshape(b_, h_, n_qc, Q_CHUNK, topk), 2, 0)
    bi = jnp.arange(b_)[:, None, None, None]
    hi = jnp.arange(h_)[None, :, None, None]
    offs = jnp.arange(L)

    def one_chunk(args):
        c, qi, si = args
        t = c * Q_CHUNK + jnp.arange(Q_CHUNK)
        own = (c * Q_CHUNK) // L
        k_own = lax.dynamic_index_in_dim(kb, own, axis=2, keepdims=False)
        v_own = lax.dynamic_index_in_dim(vb, own, axis=2, keepdims=False)
        k_sel = kb[bi, hi, si]
        v_sel = vb[bi, hi, si]
        s_sel = si[..., None] * L + offs
        sc_sel = (jnp.einsum('bhtd,bhtnld->bhtnl', qi, k_sel).astype(f32) * scale
                  - slopes[:, None, None, None] * (t[:, None, None] - s_sel).astype(f32))
        valid = jnp.arange(topk)[None, :] < (t // L)[:, None]
        sc_sel = jnp.where(valid[:, :, None], sc_sel, -jnp.inf)
        s_own = own * L + offs
        sc_own = (jnp.einsum('bhtd,bhsd->bhts', qi, k_own).astype(f32) * scale
                  - slopes[:, None, None] * (t[:, None] - s_own[None, :]).astype(f32))
        sc_own = jnp.where(s_own[None, :] <= t[:, None], sc_own, -jnp.inf)
        scores = jnp.concatenate([sc_sel.reshape(b_, h_, Q_CHUNK, topk * L), sc_own], axis=-1)
        p = jax.nn.softmax(scores, axis=-1)
        p_sel = p[..., :topk * L].reshape(b_, h_, Q_CHUNK, topk, L)
        p_own = p[..., topk * L:]
        o = (jnp.einsum('bhtnl,bhtnld->bhtd', p_sel, v_sel.astype(f32))
             + jnp.einsum('bhts,bhsd->bhtd', p_own, v_own.astype(f32)))
        return o.astype(q.dtype)

    o = lax.map(one_chunk, (jnp.arange(n_qc), qc_all, sel_all))
    return jnp.moveaxis(o, 0, 2).reshape(b_, h_, s_, d_)


def hybrid_mixer(h, w_in, lower_bound, hgrn_norm_gain, w_out):
    proj = h @ w_in
    cuts = np.cumsum([HGRN_F_DIM, HGRN_F_DIM, HGRN_WIDTH, HGRN_WIDTH,
                      MOBA_WIDTH, MOBA_WIDTH]).tolist()
    hq, hf, hi, hg, mq, mk, mv = jnp.split(proj, cuts, axis=-1)
    o_hgrn = hgrn2_mixer(hq, hf, hi, hg, lower_bound, hgrn_norm_gain)
    o_moba = _merge(moba_attention(_heads(mq, N_MOBA_HEADS), _heads(mk, N_MOBA_HEADS),
                                   _heads(mv, N_MOBA_HEADS)))
    return jnp.concatenate([o_hgrn, o_moba], axis=-1) @ w_out


def setup_inputs(seed: int = 0) -> dict:
    key = jax.random.key(seed)
    ks = jax.random.split(key, 16)
    f32 = jnp.float32

    def dense(k, shape):
        return jax.random.normal(k, shape, f32) * shape[-2] ** -0.5

    def gain(k, shape):
        return 1.0 + 0.02 * jax.random.normal(k, shape, f32)

    return {
        'x': jax.random.normal(ks[0], (BATCH, SEQ, D_MODEL), f32),
        'ffn1_norm': gain(ks[1], (DEPTH, D_MODEL)),
        'ffn1_w_gate': dense(ks[2], (DEPTH, D_MODEL, D_FF)),
        'ffn1_w_up': dense(ks[3], (DEPTH, D_MODEL, D_FF)),
        'ffn1_w_down': dense(ks[4], (DEPTH, D_FF, D_MODEL)),
        'mix_norm': gain(ks[5], (DEPTH, D_MODEL)),
        'w_in': dense(ks[6], (DEPTH, D_MODEL, D_IN_PROJ)),
        'hgrn_lower_bounds': 0.1 * jax.random.normal(ks[7], (DEPTH + 1, HGRN_F_DIM), f32),
        'hgrn_out_norm': gain(ks[8], (DEPTH, HGRN_DV)),
        'w_out': dense(ks[9], (DEPTH, D_MIX, D_MODEL)),
        'ffn2_norm': gain(ks[10], (DEPTH, D_MODEL)),
        'ffn2_w_gate': dense(ks[11], (DEPTH, D_MODEL, D_FF)),
        'ffn2_w_up': dense(ks[12], (DEPTH, D_MODEL, D_FF)),
        'ffn2_w_down': dense(ks[13], (DEPTH, D_FF, D_MODEL)),
        'final_norm': gain(ks[14], (D_MODEL,)),
    }


def reference(x, ffn1_norm, ffn1_w_gate, ffn1_w_up, ffn1_w_down, mix_norm, w_in,
              hgrn_lower_bounds, hgrn_out_norm, w_out, ffn2_norm, ffn2_w_gate,
              ffn2_w_up, ffn2_w_down, final_norm):
    lb_all = jnp.cumsum(jax.nn.softmax(hgrn_lower_bounds.astype(jnp.float32), axis=0), axis=0)
    for layer in range(DEPTH):
        h = rmsnorm(x, ffn1_norm[layer])
        x = x + MACARON_WEIGHT * swiglu(h, ffn1_w_gate[layer], ffn1_w_up[layer], ffn1_w_down[layer])
        h = rmsnorm(x, mix_norm[layer])
        x = x + hybrid_mixer(h, w_in[layer], lb_all[layer], hgrn_out_norm[layer], w_out[layer])
        h = rmsnorm(x, ffn2_norm[layer])
        x = x + MACARON_WEIGHT * swiglu(h, ffn2_w_gate[layer], ffn2_w_up[layer], ffn2_w_down[layer])
    return rmsnorm(x, final_norm)
```

```python
import functools

import jax
import jax.numpy as jnp
from jax import lax
from jax.experimental import pallas as pl
from jax.experimental.pallas import tpu as pltpu

F32 = jnp.float32
BF16 = jnp.bfloat16

HEAD = 128
N_HEADS = 16
MIX_WIDTH = N_HEADS * HEAD
MOBA_BLOCK = 256
MOBA_TOPK = 3
RMS_EPS = 1e-6
MACARON_WEIGHT = 0.5

HGRN_CHUNK = 64
HGRN_SUB = 16
FF_TILE = 256
NORM_ROWS = 16

V7X_VMEM_BYTES = 64 * 1024 * 1024
VMEM_LIMIT = V7X_VMEM_BYTES - 6 * 1024 * 1024


def _dot(a, b, precision=None):
    return jnp.dot(a, b, preferred_element_type=F32, precision=precision)


def _dot_nt(a, b, precision=None):
    return lax.dot_general(a, b, (((1,), (1,)), ((), ())),
                           preferred_element_type=F32, precision=precision)


def _dot_tn(a, b):
    return lax.dot_general(a, b, (((0,), (0,)), ((), ())), preferred_element_type=F32)


def _rmsnorm_rows(x_ref, g_ref, h_ref, copy_ref=None):
    rows = x_ref.shape[0]
    gain = g_ref[...]

    def body(i, carry):
        sl = pl.ds(pl.multiple_of(i * NORM_ROWS, NORM_ROWS), NORM_ROWS)
        x = x_ref[sl, :]
        r = lax.rsqrt(jnp.mean(x * x, axis=-1, keepdims=True) + RMS_EPS)
        h_ref[sl, :] = ((x * r) * gain).astype(BF16)
        if copy_ref is not None:
            copy_ref[sl, :] = x
        return carry

    lax.fori_loop(0, rows // NORM_ROWS, body, 0)


def _ffn_kernel(*refs, final_norm):
    if final_norm:
        x_ref, g_ref, wgu_ref, wd_ref, gf_ref, o_ref, h_ref = refs
    else:
        x_ref, g_ref, wgu_ref, wd_ref, o_ref, h_ref = refs
    f = pl.program_id(1)

    @pl.when(f == 0)
    def _():
        _rmsnorm_rows(x_ref, g_ref, h_ref, copy_ref=o_ref)

    gu = _dot(h_ref[...], wgu_ref[...])
    gate = gu[:, :FF_TILE]
    up = gu[:, FF_TILE:]
    act = (gate * jax.nn.sigmoid(gate)) * up * MACARON_WEIGHT
    o_ref[...] += _dot(act.astype(BF16), wd_ref[...])

    if final_norm:
        @pl.when(f == pl.num_programs(1) - 1)
        def _():
            gain = gf_ref[...]
            rows = o_ref.shape[0]

            def body(i, carry):
                sl = pl.ds(pl.multiple_of(i * NORM_ROWS, NORM_ROWS), NORM_ROWS)
                y = o_ref[sl, :]
                r = lax.rsqrt(jnp.mean(y * y, axis=-1, keepdims=True) + RMS_EPS)
                o_ref[sl, :] = (y * r) * gain
                return carry

            lax.fori_loop(0, rows // NORM_ROWS, body, 0)


def _ffn(x, gain, wgu, wd, final_gain=None, *, tm=512):
    t, d = x.shape
    nf = wgu.shape[0]
    final_norm = final_gain is not None
    in_specs = [
        pl.BlockSpec((tm, d), lambda i, f: (i, 0)),
        pl.BlockSpec((1, d), lambda i, f: (0, 0)),
        pl.BlockSpec((None, d, 2 * FF_TILE), lambda i, f: (f, 0, 0)),
        pl.BlockSpec((FF_TILE, d), lambda i, f: (f, 0)),
    ]
    args = [x, gain.reshape(1, d), wgu, wd]
    if final_norm:
        in_specs.append(pl.BlockSpec((1, d), lambda i, f: (0, 0)))
        args.append(final_gain.reshape(1, d))
    return pl.pallas_call(
        functools.partial(_ffn_kernel, final_norm=final_norm),
        grid=(t // tm, nf),
        in_specs=in_specs,
        out_specs=pl.BlockSpec((tm, d), lambda i, f: (i, 0)),
        out_shape=jax.ShapeDtypeStruct((t, d), F32),
        scratch_shapes=[pltpu.VMEM((tm, d), BF16)],
        compiler_params=pltpu.CompilerParams(
            dimension_semantics=("parallel", "arbitrary"),
            vmem_limit_bytes=VMEM_LIMIT),
        name="ffn_final" if final_norm else "ffn",
    )(*args)


def _prep_gate_up(w_gate, w_up):
    d, d_ff = w_gate.shape
    nf = d_ff // FF_TILE
    wg = w_gate.astype(BF16).reshape(d, nf, FF_TILE)
    wu = w_up.astype(BF16).reshape(d, nf, FF_TILE)
    return jnp.concatenate([wg, wu], axis=-1).transpose(1, 0, 2)


def _in_proj_kernel(x_ref, g_ref, w_ref, o_ref, h_ref):
    @pl.when(pl.program_id(1) == 0)
    def _():
        _rmsnorm_rows(x_ref, g_ref, h_ref)

    o_ref[...] = _dot(h_ref[...], w_ref[...])


def _in_proj(x, gain, w, *, tm=512, tn=1024):
    t, d = x.shape
    n = w.shape[1]
    return pl.pallas_call(
        _in_proj_kernel,
        grid=(t // tm, n // tn),
        in_specs=[
            pl.BlockSpec((tm, d), lambda i, j: (i, 0)),
            pl.BlockSpec((1, d), lambda i, j: (0, 0)),
            pl.BlockSpec((d, tn), lambda i, j: (0, j)),
        ],
        out_specs=pl.BlockSpec((tm, tn), lambda i, j: (i, j)),
        out_shape=jax.ShapeDtypeStruct((t, n), F32),
        scratch_shapes=[pltpu.VMEM((tm, d), BF16)],
        compiler_params=pltpu.CompilerParams(
            dimension_semantics=("parallel", "arbitrary"),
            vmem_limit_bytes=VMEM_LIMIT),
        name="in_proj",
    )(x, gain.reshape(1, d), w)


def _out_proj_kernel(oh_ref, om_ref, wh_ref, wm_ref, x_ref, o_ref):
    o_ref[...] = x_ref[...] + _dot(oh_ref[...], wh_ref[...]) + _dot(om_ref[...], wm_ref[...])


def _out_proj(o_hgrn, o_moba, w, x, *, tm=1024, tn=1024):
    t, d = x.shape
    kh = o_hgrn.shape[1]
    return pl.pallas_call(
        _out_proj_kernel,
        grid=(t // tm, d // tn),
        in_specs=[
            pl.BlockSpec((tm, kh), lambda i, j: (i, 0)),
            pl.BlockSpec((tm, kh), lambda i, j: (i, 0)),
            pl.BlockSpec((kh, tn), lambda i, j: (0, j)),
            pl.BlockSpec((kh, tn), lambda i, j: (1, j)),
            pl.BlockSpec((tm, tn), lambda i, j: (i, j)),
        ],
        out_specs=pl.BlockSpec((tm, tn), lambda i, j: (i, j)),
        out_shape=jax.ShapeDtypeStruct((t, d), F32),
        compiler_params=pltpu.CompilerParams(
            dimension_semantics=("parallel", "arbitrary"),
            vmem_limit_bytes=VMEM_LIMIT),
        name="out_proj",
    )(o_hgrn, o_moba, w, w, x)


def _hgrn_kernel(hq_ref, hf_ref, hi_ref, hg_ref, lb_ref, gain_ref, o_ref,
                 st_ref, cum_ref, q_ref, k_ref):
    c, sub = HGRN_CHUNK, HGRN_SUB
    seq = hq_ref.shape[0]
    lb = lb_ref[...]
    gain = gain_ref[...]
    tri = (lax.broadcasted_iota(jnp.int32, (c, c), 0)
           >= lax.broadcasted_iota(jnp.int32, (c, c), 1)).astype(F32)
    lane = lax.broadcasted_iota(jnp.int32, (sub, c), 1)
    trow = lax.broadcasted_iota(jnp.int32, (sub, HEAD), 0)
    st_ref[...] = jnp.zeros_like(st_ref)

    def chunk(ci, carry):
        sl = pl.ds(pl.multiple_of(ci * c, c), c)
        hq = hq_ref[sl, :]
        forget = lb + (1.0 - lb) * jax.nn.sigmoid(hf_ref[sl, :])
        logf = jnp.log(forget)
        cum = _dot(tri, logf, precision=lax.Precision.HIGHEST)
        q = (hq * jax.nn.sigmoid(hq)) * (HEAD ** -0.5)
        k = 1.0 - forget
        v = hi_ref[sl, :]
        vb = v.astype(BF16)
        cum_ref[...] = cum
        q_ref[...] = q
        k_ref[...] = k
        last = cum[c - 1:c, :]

        st = st_ref[...]
        o = _dot_nt((q * jnp.exp(cum)).astype(BF16), st.astype(BF16))
        k_end = (k * jnp.exp(last - cum)).astype(BF16)
        st_ref[...] = st * jnp.exp(last) + _dot_tn(vb, k_end)

        rows = []
        for i in range(c // sub):
            lo = i * sub
            cum_b = cum[lo:lo + sub, :]
            q_b = q[lo:lo + sub, :]
            a = jnp.zeros((sub, c), F32)
            for s in range(sub):
                r = lo + s
                diff = jnp.where(trow >= s, cum_b - cum_ref[r:r + 1, :], -jnp.inf)
                p = (q_b * jnp.exp(diff)) * k_ref[r:r + 1, :]
                a = jnp.where(lane == r, jnp.sum(p, axis=-1, keepdims=True), a)
            if i > 0:
                edge = cum_ref[lo - 1:lo, :]
                q_t = (q_b * jnp.exp(cum_b - edge)).astype(BF16)
                k_t = (k * jnp.exp(jnp.minimum(edge - cum, 0.0))).astype(BF16)
                a = jnp.where(lane < lo, _dot_nt(q_t, k_t), a)
            rows.append(a)
        scores = jnp.concatenate(rows, axis=0)
        o = o + _dot(scores.astype(BF16), vb)

        r = lax.rsqrt(jnp.mean(o * o, axis=-1, keepdims=True) + RMS_EPS)
        hg = hg_ref[sl, :]
        o_ref[sl, :] = (((o * r) * gain) * (hg * jax.nn.sigmoid(hg))).astype(o_ref.dtype)
        return carry

    lax.fori_loop(0, seq // c, chunk, 0)


def _hgrn(proj, lower_bound, norm_gain):
    b, s, _ = proj.shape
    col = lambda off: pl.BlockSpec((None, s, HEAD), lambda bi, hi: (bi, 0, off + hi))
    return pl.pallas_call(
        _hgrn_kernel,
        grid=(b, N_HEADS),
        in_specs=[
            col(0), col(N_HEADS), col(2 * N_HEADS), col(3 * N_HEADS),
            pl.BlockSpec((None, 1, HEAD), lambda bi, hi: (hi, 0, 0)),
            pl.BlockSpec((1, HEAD), lambda bi, hi: (0, 0)),
        ],
        out_specs=pl.BlockSpec((None, s, HEAD), lambda bi, hi: (bi, 0, hi)),
        out_shape=jax.ShapeDtypeStruct((b, s, MIX_WIDTH), BF16),
        scratch_shapes=[
            pltpu.VMEM((HEAD, HEAD), F32),
            pltpu.VMEM((HGRN_CHUNK, HEAD), F32),
            pltpu.VMEM((HGRN_CHUNK, HEAD), F32),
            pltpu.VMEM((HGRN_CHUNK, HEAD), F32),
        ],
        compiler_params=pltpu.CompilerParams(
            dimension_semantics=("parallel", "parallel"),
            vmem_limit_bytes=VMEM_LIMIT),
        name="hgrn2",
    )(proj, proj, proj, proj, lower_bound.reshape(N_HEADS, 1, HEAD), norm_gain.reshape(1, HEAD))


def _moba_kernel(slopes_ref, q_ref, k_ref, v_ref, o_ref,
                 kb_ref, vb_ref, kmean_ref, m_ref, l_ref, acc_ref):
    blk = MOBA_BLOCK
    seq = q_ref.shape[0]
    nb = seq // blk
    topk = min(MOBA_TOPK, max(nb - 1, 1))
    scale = HEAD ** -0.5
    slope = slopes_ref[pl.program_id(1)]

    kmean_ref[...] = jnp.zeros_like(kmean_ref)
    for n in range(nb):
        kblk = k_ref[n * blk:(n + 1) * blk, :]
        kmean_ref[n:n + 1, :] = jnp.mean(kblk, axis=0, keepdims=True)
        kb_ref[n * blk:(n + 1) * blk, :] = kblk.astype(BF16)
        vb_ref[n * blk:(n + 1) * blk, :] = v_ref[n * blk:(n + 1) * blk, :].astype(BF16)

    rel = (lax.broadcasted_iota(jnp.int32, (blk, blk), 1)
           - lax.broadcasted_iota(jnp.int32, (blk, blk), 0))
    bias_own = slope * rel.astype(F32)
    blk_lane = lax.broadcasted_iota(jnp.int32, (blk, HEAD), 1)

    def q_block(qb, carry):
        qsl = pl.ds(pl.multiple_of(qb * blk, blk), blk)
        q = q_ref[qsl, :]
        gate = _dot_nt(q, kmean_ref[...], precision=lax.Precision.HIGHEST)
        gate = jnp.where(blk_lane < qb, gate, -jnp.inf)
        sel = jnp.zeros((blk, HEAD), F32)
        for _ in range(topk):
            best = jnp.max(gate, axis=-1, keepdims=True)
            first = jnp.min(jnp.where(gate == best, blk_lane, HEAD), axis=-1, keepdims=True)
            first = jnp.where(best > -jnp.inf, first, -1)
            pick = blk_lane == first
            sel = jnp.where(pick, 1.0, sel)
            gate = jnp.where(pick, -jnp.inf, gate)

        qb16 = q.astype(BF16)
        s = _dot_nt(qb16, kb_ref[qsl, :]) * scale + bias_own
        s = jnp.where(rel <= 0, s, -jnp.inf)
        m = jnp.max(s, axis=-1, keepdims=True)
        p = jnp.exp(s - m)
        m_ref[...] = m
        l_ref[...] = jnp.sum(p, axis=-1, keepdims=True)
        acc_ref[...] = _dot(p.astype(BF16), vb_ref[qsl, :])

        for n in range(nb - 1):
            @pl.when(n < qb)
            def _():
                offset = ((n - qb) * blk).astype(F32)
                s = (_dot_nt(qb16, kb_ref[n * blk:(n + 1) * blk, :]) * scale
                     + (bias_own + slope * offset))
                s = jnp.where(sel[:, n:n + 1] > 0.0, s, -jnp.inf)
                m_old = m_ref[...]
                m_new = jnp.maximum(m_old, jnp.max(s, axis=-1, keepdims=True))
                alpha = jnp.exp(m_old - m_new)
                p = jnp.exp(s - m_new)
                l_ref[...] = alpha * l_ref[...] + jnp.sum(p, axis=-1, keepdims=True)
                acc_ref[...] = alpha * acc_ref[...] + _dot(p.astype(BF16),
                                                           vb_ref[n * blk:(n + 1) * blk, :])
                m_ref[...] = m_new

        o_ref[qsl, :] = (acc_ref[...] / l_ref[...]).astype(o_ref.dtype)
        return carry

    lax.fori_loop(0, nb, q_block, 0)


def _moba(proj):
    b, s, _ = proj.shape
    assert s % MOBA_BLOCK == 0 and s // MOBA_BLOCK <= HEAD
    slopes = jnp.exp2(-8.0 * jnp.arange(1, N_HEADS + 1, dtype=F32) / N_HEADS)
    col = lambda off: pl.BlockSpec((None, s, HEAD), lambda bi, hi: (bi, 0, off + hi))
    return pl.pallas_call(
        _moba_kernel,
        grid=(b, N_HEADS),
        in_specs=[
            pl.BlockSpec(memory_space=pltpu.SMEM),
            col(4 * N_HEADS), col(5 * N_HEADS), col(6 * N_HEADS),
        ],
        out_specs=pl.BlockSpec((None, s, HEAD), lambda bi, hi: (bi, 0, hi)),
        out_shape=jax.ShapeDtypeStruct((b, s, MIX_WIDTH), BF16),
        scratch_shapes=[
            pltpu.VMEM((s, HEAD), BF16),
            pltpu.VMEM((s, HEAD), BF16),
            pltpu.VMEM((HEAD, HEAD), F32),
            pltpu.VMEM((MOBA_BLOCK, 1), F32),
            pltpu.VMEM((MOBA_BLOCK, 1), F32),
            pltpu.VMEM((MOBA_BLOCK, HEAD), F32),
        ],
        compiler_params=pltpu.CompilerParams(
            dimension_semantics=("parallel", "parallel"),
            vmem_limit_bytes=VMEM_LIMIT),
        name="moba",
    )(slopes, proj, proj, proj)


def kernel(x, ffn1_norm, ffn1_w_gate, ffn1_w_up, ffn1_w_down, mix_norm, w_in, hgrn_lower_bounds, hgrn_out_norm, w_out, ffn2_norm, ffn2_w_gate, ffn2_w_up, ffn2_w_down, final_norm):
    b, s, d = x.shape
    depth = ffn1_norm.shape[0]
    lb_all = jnp.cumsum(jax.nn.softmax(hgrn_lower_bounds.astype(F32), axis=0), axis=0)
    xs = x.reshape(b * s, d)
    for layer in range(depth):
        xs = _ffn(xs, ffn1_norm[layer], _prep_gate_up(ffn1_w_gate[layer], ffn1_w_up[layer]),
                  ffn1_w_down[layer].astype(BF16))
        proj = _in_proj(xs, mix_norm[layer], w_in[layer].astype(BF16)).reshape(b, s, -1)
        o_hgrn = _hgrn(proj, lb_all[layer], hgrn_out_norm[layer]).reshape(b * s, MIX_WIDTH)
        o_moba = _moba(proj).reshape(b * s, MIX_WIDTH)
        xs = _out_proj(o_hgrn, o_moba, w_out[layer].astype(BF16), xs)
        last = layer == depth - 1
        xs = _ffn(xs, ffn2_norm[layer], _prep_gate_up(ffn2_w_gate[layer], ffn2_w_up[layer]),
                  ffn2_w_down[layer].astype(BF16), final_norm if last else None)
    if depth == 0:
        raise NotImplementedError("depth 0 has no FFN to carry the final norm")
    return xs.reshape(b, s, d)
```

```python
import functools

import jax
import jax.numpy as jnp
from jax import lax
from jax.experimental import pallas as pl
from jax.experimental.pallas import tpu as pltpu

F32 = jnp.float32
BF16 = jnp.bfloat16

HEAD = 128
N_HEADS = 16
MIX_WIDTH = N_HEADS * HEAD
MOBA_BLOCK = 256
MOBA_TOPK = 3
RMS_EPS = 1e-6
MACARON_WEIGHT = 0.5

HGRN_CHUNK = 64
HGRN_SUB = 16
HGRN_GROUP = 4
HGRN_SEQ_BLOCK = 1024
FF_TILE = 256
NORM_ROWS = 16

V7X_VMEM_BYTES = 64 * 1024 * 1024
VMEM_LIMIT = V7X_VMEM_BYTES - 6 * 1024 * 1024


def _dot(a, b, precision=None):
    return jnp.dot(a, b, preferred_element_type=F32, precision=precision)


def _dot_nt(a, b, precision=None):
    return lax.dot_general(a, b, (((1,), (1,)), ((), ())),
                           preferred_element_type=F32, precision=precision)


def _dot_tn(a, b):
    return lax.dot_general(a, b, (((0,), (0,)), ((), ())), preferred_element_type=F32)


def _rmsnorm_rows(x_ref, g_ref, h_ref, copy_ref=None):
    rows = x_ref.shape[0]
    gain = g_ref[...]

    def body(i, carry):
        sl = pl.ds(pl.multiple_of(i * NORM_ROWS, NORM_ROWS), NORM_ROWS)
        x = x_ref[sl, :]
        r = lax.rsqrt(jnp.mean(x * x, axis=-1, keepdims=True) + RMS_EPS)
        h_ref[sl, :] = ((x * r) * gain).astype(BF16)
        if copy_ref is not None:
            copy_ref[sl, :] = x
        return carry

    lax.fori_loop(0, rows // NORM_ROWS, body, 0)


def _ffn_kernel(*refs, final_norm):
    if final_norm:
        x_ref, g_ref, wgu_ref, wd_ref, gf_ref, o_ref, h_ref, act_ref = refs
    else:
        x_ref, g_ref, wgu_ref, wd_ref, o_ref, h_ref, act_ref = refs
    f = pl.program_id(1)
    last = pl.num_programs(1) - 1

    def gate_up():
        gu = _dot(h_ref[...], wgu_ref[...])
        gate = gu[:, :FF_TILE]
        up = gu[:, FF_TILE:]
        return ((gate * jax.nn.sigmoid(gate)) * up * MACARON_WEIGHT).astype(BF16)

    @pl.when(f == 0)
    def _():
        _rmsnorm_rows(x_ref, g_ref, h_ref, copy_ref=o_ref)
        act_ref[...] = gate_up()

    @pl.when(jnp.logical_and(f > 0, f < last))
    def _():
        o_ref[...] += _dot(act_ref[...], wd_ref[...])
        act_ref[...] = gate_up()

    @pl.when(f == last)
    def _():
        o_ref[...] += _dot(act_ref[...], wd_ref[...])
        if final_norm:
            gain = gf_ref[...]

            def body(i, carry):
                sl = pl.ds(pl.multiple_of(i * NORM_ROWS, NORM_ROWS), NORM_ROWS)
                y = o_ref[sl, :]
                r = lax.rsqrt(jnp.mean(y * y, axis=-1, keepdims=True) + RMS_EPS)
                o_ref[sl, :] = (y * r) * gain
                return carry

            lax.fori_loop(0, o_ref.shape[0] // NORM_ROWS, body, 0)


def _ffn(x, gain, wgu, wd, final_gain=None, *, tm=512):
    t, d = x.shape
    nf = wgu.shape[0]
    final_norm = final_gain is not None
    in_specs = [
        pl.BlockSpec((tm, d), lambda i, f: (i, 0)),
        pl.BlockSpec((1, d), lambda i, f: (0, 0)),
        pl.BlockSpec((None, d, 2 * FF_TILE), lambda i, f: (jnp.minimum(f, nf - 1), 0, 0)),
        pl.BlockSpec((FF_TILE, d), lambda i, f: (jnp.maximum(f - 1, 0), 0)),
    ]
    args = [x, gain.reshape(1, d), wgu, wd]
    if final_norm:
        in_specs.append(pl.BlockSpec((1, d), lambda i, f: (0, 0)))
        args.append(final_gain.reshape(1, d))
    return pl.pallas_call(
        functools.partial(_ffn_kernel, final_norm=final_norm),
        grid=(t // tm, nf + 1),
        in_specs=in_specs,
        out_specs=pl.BlockSpec((tm, d), lambda i, f: (i, 0)),
        out_shape=jax.ShapeDtypeStruct((t, d), F32),
        scratch_shapes=[pltpu.VMEM((tm, d), BF16), pltpu.VMEM((tm, FF_TILE), BF16)],
        compiler_params=pltpu.CompilerParams(
            dimension_semantics=("parallel", "arbitrary"),
            vmem_limit_bytes=VMEM_LIMIT),
        name="ffn_final" if final_norm else "ffn",
    )(*args)


def _pack_gate_up_kernel(wg_ref, wu_ref, o_ref):
    o_ref[:, :FF_TILE] = wg_ref[...].astype(BF16)
    o_ref[:, FF_TILE:] = wu_ref[...].astype(BF16)


def _prep_gate_up(w_gate, w_up):
    d, d_ff = w_gate.shape
    nf = d_ff // FF_TILE
    col = pl.BlockSpec((d, FF_TILE), lambda f: (0, f))
    return pl.pallas_call(
        _pack_gate_up_kernel,
        grid=(nf,),
        in_specs=[col, col],
        out_specs=pl.BlockSpec((None, d, 2 * FF_TILE), lambda f: (f, 0, 0)),
        out_shape=jax.ShapeDtypeStruct((nf, d, 2 * FF_TILE), BF16),
        compiler_params=pltpu.CompilerParams(
            dimension_semantics=("parallel",), vmem_limit_bytes=VMEM_LIMIT),
        name="pack_gate_up",
    )(w_gate, w_up)


def _in_proj_kernel(x_ref, g_ref, w_ref, o_ref, h_ref):
    @pl.when(pl.program_id(1) == 0)
    def _():
        _rmsnorm_rows(x_ref, g_ref, h_ref)

    o_ref[...] = _dot(h_ref[...], w_ref[...])


def _in_proj(x, gain, w, *, tm=1024, tn=1024):
    t, d = x.shape
    n = w.shape[1]
    return pl.pallas_call(
        _in_proj_kernel,
        grid=(t // tm, n // tn),
        in_specs=[
            pl.BlockSpec((tm, d), lambda i, j: (i, 0), pipeline_mode=pl.Buffered(1)),
            pl.BlockSpec((1, d), lambda i, j: (0, 0)),
            pl.BlockSpec((d, tn), lambda i, j: (0, j)),
        ],
        out_specs=pl.BlockSpec((tm, tn), lambda i, j: (i, j)),
        out_shape=jax.ShapeDtypeStruct((t, n), F32),
        scratch_shapes=[pltpu.VMEM((tm, d), BF16)],
        compiler_params=pltpu.CompilerParams(
            dimension_semantics=("parallel", "arbitrary"),
            vmem_limit_bytes=VMEM_LIMIT),
        name="in_proj",
    )(x, gain.reshape(1, d), w)


def _out_proj_kernel(oh_ref, om_ref, wh_ref, wm_ref, x_ref, o_ref):
    o_ref[...] = x_ref[...] + _dot(oh_ref[...], wh_ref[...]) + _dot(om_ref[...], wm_ref[...])


def _out_proj(o_hgrn, o_moba, w, x, *, tm=1024, tn=1024):
    t, d = x.shape
    kh = o_hgrn.shape[1]
    return pl.pallas_call(
        _out_proj_kernel,
        grid=(t // tm, d // tn),
        in_specs=[
            pl.BlockSpec((tm, kh), lambda i, j: (i, 0)),
            pl.BlockSpec((tm, kh), lambda i, j: (i, 0)),
            pl.BlockSpec((kh, tn), lambda i, j: (0, j)),
            pl.BlockSpec((kh, tn), lambda i, j: (1, j)),
            pl.BlockSpec((tm, tn), lambda i, j: (i, j)),
        ],
        out_specs=pl.BlockSpec((tm, tn), lambda i, j: (i, j)),
        out_shape=jax.ShapeDtypeStruct((t, d), F32),
        compiler_params=pltpu.CompilerParams(
            dimension_semantics=("parallel", "arbitrary"),
            vmem_limit_bytes=VMEM_LIMIT),
        name="out_proj",
    )(o_hgrn, o_moba, w, w, x)


def _hgrn_kernel(hq_ref, hf_ref, hi_ref, hg_ref, lb_ref, gain_ref, o_ref,
                 st_ref, cum_ref, k_ref):
    c, sub, half = HGRN_CHUNK, HGRN_SUB, HGRN_SUB // 2
    rows_total = hq_ref.shape[0]
    lb_all = lb_ref[...]
    gain = gain_ref[...]
    tri = (lax.broadcasted_iota(jnp.int32, (c, c), 0)
           >= lax.broadcasted_iota(jnp.int32, (c, c), 1)).astype(BF16)
    lane = lax.broadcasted_iota(jnp.int32, (half, c), 1)
    trow = lax.broadcasted_iota(jnp.int32, (half, HEAD), 0)

    @pl.when(pl.program_id(2) == 0)
    def _():
        st_ref[...] = jnp.zeros_like(st_ref)

    def chunk(ci, carry):
        sl = pl.ds(pl.multiple_of(ci * c, c), c)
        forget_all = lb_all + (1.0 - lb_all) * jax.nn.sigmoid(hf_ref[sl, :])
        logf = jnp.log(forget_all)
        hi = logf.astype(BF16)
        rest = logf - hi.astype(F32)
        mid = rest.astype(BF16)
        lo = (rest - mid.astype(F32)).astype(BF16)
        cum_all = _dot(tri, hi) + _dot(tri, mid) + _dot(tri, lo)

        for g in range(HGRN_GROUP):
            cols = slice(g * HEAD, (g + 1) * HEAD)
            cum = cum_all[:, cols]
            forget = forget_all[:, cols]
            hq = hq_ref[sl, cols]
            q = (hq * jax.nn.sigmoid(hq)) * (HEAD ** -0.5)
            k = 1.0 - forget
            vb = hi_ref[sl, cols].astype(BF16)
            cum_ref[g] = cum
            k_ref[g] = k
            last = cum[c - 1:c, :]

            st = st_ref[g]
            o = _dot_nt((q * jnp.exp(cum)).astype(BF16), st.astype(BF16))
            k_end = (k * jnp.exp(last - cum)).astype(BF16)
            st_ref[g] = st * jnp.exp(last) + _dot_tn(vb, k_end)

            blocks = []
            for i in range(c // sub):
                lo_row = i * sub
                off = None
                if i > 0:
                    edge = cum_ref[g, lo_row - 1:lo_row, :]
                    q_t = (q[lo_row:lo_row + sub, :]
                           * jnp.exp(cum[lo_row:lo_row + sub, :] - edge)).astype(BF16)
                    k_t = (k[:lo_row, :] * jnp.exp(edge - cum[:lo_row, :])).astype(BF16)
                    k_t = jnp.concatenate([k_t, jnp.zeros((c - lo_row, HEAD), BF16)], axis=0)
                    off = _dot_nt(q_t, k_t)
                for h in range(2):
                    top = lo_row + h * half
                    cum_b = cum[top:top + half, :]
                    q_b = q[top:top + half, :]
                    a = jnp.zeros((half, c), F32)
                    for s in range((h + 1) * half):
                        r = lo_row + s
                        diff = cum_b - cum_ref[g, r:r + 1, :]
                        if s > h * half:
                            diff = jnp.where(trow >= s - h * half, diff, -jnp.inf)
                        p = (q_b * jnp.exp(diff)) * k_ref[g, r:r + 1, :]
                        a = jnp.where(lane == r, jnp.sum(p, axis=-1, keepdims=True), a)
                    if off is not None:
                        a = a + off[h * half:(h + 1) * half, :]
                    blocks.append(a)
            scores = jnp.concatenate(blocks, axis=0)
            o = o + _dot(scores.astype(BF16), vb)

            r = lax.rsqrt(jnp.mean(o * o, axis=-1, keepdims=True) + RMS_EPS)
            hg = hg_ref[sl, cols]
            o_ref[sl, cols] = (((o * r) * gain) * (hg * jax.nn.sigmoid(hg))).astype(o_ref.dtype)
        return carry

    lax.fori_loop(0, rows_total // c, chunk, 0)


def _hgrn(proj, lower_bound, norm_gain):
    b, s, _ = proj.shape
    grp = HGRN_GROUP
    ng = N_HEADS // grp
    sb = min(HGRN_SEQ_BLOCK, s)
    col = lambda off: pl.BlockSpec((None, sb, grp * HEAD), lambda bi, gi, si: (bi, si, off + gi))
    return pl.pallas_call(
        _hgrn_kernel,
        grid=(b, ng, s // sb),
        in_specs=[
            col(0), col(ng), col(2 * ng), col(3 * ng),
            pl.BlockSpec((None, 1, grp * HEAD), lambda bi, gi, si: (gi, 0, 0)),
            pl.BlockSpec((1, HEAD), lambda bi, gi, si: (0, 0)),
        ],
        out_specs=pl.BlockSpec((None, sb, grp * HEAD), lambda bi, gi, si: (bi, si, gi)),
        out_shape=jax.ShapeDtypeStruct((b, s, MIX_WIDTH), BF16),
        scratch_shapes=[
            pltpu.VMEM((grp, HEAD, HEAD), F32),
            pltpu.VMEM((grp, HGRN_CHUNK, HEAD), F32),
            pltpu.VMEM((grp, HGRN_CHUNK, HEAD), F32),
        ],
        compiler_params=pltpu.CompilerParams(
            dimension_semantics=("parallel", "parallel", "arbitrary"),
            vmem_limit_bytes=VMEM_LIMIT),
        name="hgrn2",
    )(proj, proj, proj, proj, lower_bound.reshape(ng, 1, grp * HEAD), norm_gain.reshape(1, HEAD))


def _moba_kernel(slopes_ref, q_ref, k_ref, v_ref, o_ref, kb_ref, vt_ref, kmean_ref):
    blk = MOBA_BLOCK
    seq = q_ref.shape[0]
    nb = seq // blk
    topk = min(MOBA_TOPK, max(nb - 1, 1))
    scale = HEAD ** -0.5
    slope = slopes_ref[pl.program_id(1)]
    neg_inf = jnp.full((1, blk), -jnp.inf, F32)

    kmean_ref[...] = jnp.zeros_like(kmean_ref)
    for n in range(nb):
        rows = slice(n * blk, (n + 1) * blk)
        kblk = k_ref[rows, :]
        kmean_ref[n:n + 1, :] = jnp.mean(kblk, axis=0, keepdims=True)
        kb_ref[rows, :] = kblk.astype(BF16)
        vt_ref[:, rows] = v_ref[rows, :].T.astype(BF16)

    rel = (lax.broadcasted_iota(jnp.int32, (blk, blk), 0)
           - lax.broadcasted_iota(jnp.int32, (blk, blk), 1))
    bias_own = slope * rel.astype(F32)

    for qb in range(nb):
        qsl = slice(qb * blk, (qb + 1) * blk)
        qt = q_ref[qsl, :].T
        if qb <= topk:
            picked = [None] * qb
        else:
            rows_p = -(-qb // 8) * 8
            blk_row = lax.broadcasted_iota(jnp.int32, (rows_p, blk), 0)
            gate = _dot(kmean_ref[0:rows_p, :], qt, precision=lax.Precision.HIGHEST)
            gate = jnp.where(blk_row < qb, gate, -jnp.inf)
            sel = jnp.zeros((rows_p, blk), F32)
            for _ in range(topk):
                best = jnp.max(gate, axis=0, keepdims=True)
                first = jnp.min(jnp.where(gate == best, blk_row, rows_p), axis=0, keepdims=True)
                pick = blk_row == first
                sel = jnp.where(pick, 1.0, sel)
                gate = jnp.where(pick, -jnp.inf, gate)
            picked = [sel[n:n + 1, :] > 0.0 for n in range(qb)]

        nk = (qb + 1) * blk
        raw = _dot(kb_ref[0:nk, :], qt.astype(BF16))
        tiles, shifts = [], []
        for n in range(qb):
            tiles.append(raw[n * blk:(n + 1) * blk, :] * scale + bias_own)
            offset = slope * float((n - qb) * blk)
            shifts.append(offset if picked[n] is None else jnp.where(picked[n], offset, neg_inf))
        own = raw[qb * blk:nk, :] * scale + bias_own
        tiles.append(jnp.where(rel <= 0, own, -jnp.inf))
        shifts.append(0.0)

        m = None
        for t, sh in zip(tiles, shifts):
            top = jnp.max(t, axis=0, keepdims=True) + sh
            m = top if m is None else jnp.maximum(m, top)
        probs, denom = [], None
        for t, sh in zip(tiles, shifts):
            p = jnp.exp(t + (sh - m))
            probs.append(p.astype(BF16))
            part = jnp.sum(p, axis=0, keepdims=True)
            denom = part if denom is None else denom + part
        acc = _dot(vt_ref[:, 0:nk], jnp.concatenate(probs, axis=0))
        o_ref[qsl, :] = (acc / denom).T.astype(o_ref.dtype)


def _moba(proj):
    b, s, _ = proj.shape
    assert s % MOBA_BLOCK == 0
    nb = s // MOBA_BLOCK
    nbp = -(-nb // 8) * 8
    slopes = jnp.exp2(-8.0 * jnp.arange(1, N_HEADS + 1, dtype=F32) / N_HEADS)
    col = lambda off: pl.BlockSpec((None, s, HEAD), lambda bi, hi: (bi, 0, off + hi))
    return pl.pallas_call(
        _moba_kernel,
        grid=(b, N_HEADS),
        in_specs=[
            pl.BlockSpec(memory_space=pltpu.SMEM),
            col(4 * N_HEADS), col(5 * N_HEADS), col(6 * N_HEADS),
        ],
        out_specs=pl.BlockSpec((None, s, HEAD), lambda bi, hi: (bi, 0, hi)),
        out_shape=jax.ShapeDtypeStruct((b, s, MIX_WIDTH), BF16),
        scratch_shapes=[
            pltpu.VMEM((s, HEAD), BF16),
            pltpu.VMEM((HEAD, s), BF16),
            pltpu.VMEM((nbp, HEAD), F32),
        ],
        compiler_params=pltpu.CompilerParams(
            dimension_semantics=("parallel", "parallel"),
            vmem_limit_bytes=VMEM_LIMIT),
        name="moba",
    )(slopes, proj, proj, proj)


def kernel(x, ffn1_norm, ffn1_w_gate, ffn1_w_up, ffn1_w_down, mix_norm, w_in, hgrn_lower_bounds, hgrn_out_norm, w_out, ffn2_norm, ffn2_w_gate, ffn2_w_up, ffn2_w_down, final_norm):
    b, s, d = x.shape
    depth = ffn1_norm.shape[0]
    lb_all = jnp.cumsum(jax.nn.softmax(hgrn_lower_bounds.astype(F32), axis=0), axis=0)
    xs = x.reshape(b * s, d)
    for layer in range(depth):
        xs = _ffn(xs, ffn1_norm[layer], _prep_gate_up(ffn1_w_gate[layer], ffn1_w_up[layer]),
                  ffn1_w_down[layer].astype(BF16))
        proj = _in_proj(xs, mix_norm[layer], w_in[layer].astype(BF16)).reshape(b, s, -1)
        o_hgrn = _hgrn(proj, lb_all[layer], hgrn_out_norm[layer]).reshape(b * s, MIX_WIDTH)
        o_moba = _moba(proj).reshape(b * s, MIX_WIDTH)
        xs = _out_proj(o_hgrn, o_moba, w_out[layer].astype(BF16), xs)
        last = layer == depth - 1
        xs = _ffn(xs, ffn2_norm[layer], _prep_gate_up(ffn2_w_gate[layer], ffn2_w_up[layer]),
                  ffn2_w_down[layer].astype(BF16), final_norm if last else None)
    if depth == 0:
        raise NotImplementedError("depth 0 has no FFN to carry the final norm")
    return xs.reshape(b, s, d)
```

```python
import functools

import jax
import jax.numpy as jnp
from jax import lax
from jax.experimental import pallas as pl
from jax.experimental.pallas import tpu as pltpu

F32 = jnp.float32
BF16 = jnp.bfloat16

HEAD = 128
N_HEADS = 16
MIX_WIDTH = N_HEADS * HEAD
MOBA_BLOCK = 256
MOBA_TOPK = 3
RMS_EPS = 1e-6
MACARON_WEIGHT = 0.5
LOG2E = 1.4426950408889634

HGRN_CHUNK = 64
HGRN_SUB = 16
HGRN_GROUP = 4
HGRN_SEQ_BLOCK = 1024
FF_TILE = 256
NORM_ROWS = 16
NORM_UNROLL = 4

V7X_VMEM_BYTES = 64 * 1024 * 1024
VMEM_LIMIT = V7X_VMEM_BYTES - 6 * 1024 * 1024


def _dot(a, b, precision=None):
    return jnp.dot(a, b, preferred_element_type=F32, precision=precision)


def _dot_nt(a, b, precision=None):
    return lax.dot_general(a, b, (((1,), (1,)), ((), ())),
                           preferred_element_type=F32, precision=precision)


def _dot_tn(a, b):
    return lax.dot_general(a, b, (((0,), (0,)), ((), ())), preferred_element_type=F32)


def _rmsnorm_rows(x_ref, g_ref, h_ref, copy_ref=None):
    rows = x_ref.shape[0]
    gain = g_ref[...]

    def body(i, carry):
        sl = pl.ds(pl.multiple_of(i * NORM_ROWS, NORM_ROWS), NORM_ROWS)
        x = x_ref[sl, :]
        r = lax.rsqrt(jnp.mean(x * x, axis=-1, keepdims=True) + RMS_EPS)
        h_ref[sl, :] = ((x * r) * gain).astype(BF16)
        if copy_ref is not None:
            copy_ref[sl, :] = x
        return carry

    lax.fori_loop(0, rows // NORM_ROWS, body, 0, unroll=NORM_UNROLL)


def _ffn_kernel(*refs, final_norm):
    if final_norm:
        x_ref, g_ref, wgu_ref, wd_ref, gf_ref, o_ref, h_ref = refs
    else:
        x_ref, g_ref, wgu_ref, wd_ref, o_ref, h_ref = refs
    f = pl.program_id(1)

    @pl.when(f == 0)
    def _():
        _rmsnorm_rows(x_ref, g_ref, h_ref, copy_ref=o_ref)

    gu = _dot(h_ref[...], wgu_ref[...])
    gate = gu[:, :FF_TILE]
    up = gu[:, FF_TILE:]
    act = (gate * jax.nn.sigmoid(gate)) * up * MACARON_WEIGHT
    o_ref[...] += _dot(act.astype(BF16), wd_ref[...])

    if final_norm:
        @pl.when(f == pl.num_programs(1) - 1)
        def _():
            gain = gf_ref[...]

            def body(i, carry):
                sl = pl.ds(pl.multiple_of(i * NORM_ROWS, NORM_ROWS), NORM_ROWS)
                y = o_ref[sl, :]
                r = lax.rsqrt(jnp.mean(y * y, axis=-1, keepdims=True) + RMS_EPS)
                o_ref[sl, :] = (y * r) * gain
                return carry

            lax.fori_loop(0, o_ref.shape[0] // NORM_ROWS, body, 0, unroll=NORM_UNROLL)


def _ffn(x, gain, wgu, wd, final_gain=None, *, tm=512):
    t, d = x.shape
    nf = wgu.shape[0]
    final_norm = final_gain is not None
    in_specs = [
        pl.BlockSpec((tm, d), lambda i, f: (i, 0)),
        pl.BlockSpec((1, d), lambda i, f: (0, 0)),
        pl.BlockSpec((None, d, 2 * FF_TILE), lambda i, f: (f, 0, 0)),
        pl.BlockSpec((FF_TILE, d), lambda i, f: (f, 0)),
    ]
    args = [x, gain.reshape(1, d), wgu, wd]
    if final_norm:
        in_specs.append(pl.BlockSpec((1, d), lambda i, f: (0, 0)))
        args.append(final_gain.reshape(1, d))
    return pl.pallas_call(
        functools.partial(_ffn_kernel, final_norm=final_norm),
        grid=(t // tm, nf),
        in_specs=in_specs,
        out_specs=pl.BlockSpec((tm, d), lambda i, f: (i, 0)),
        out_shape=jax.ShapeDtypeStruct((t, d), F32),
        scratch_shapes=[pltpu.VMEM((tm, d), BF16)],
        compiler_params=pltpu.CompilerParams(
            dimension_semantics=("parallel", "arbitrary"),
            vmem_limit_bytes=VMEM_LIMIT),
        name="ffn_final" if final_norm else "ffn",
    )(*args)


def _pack_gate_up_kernel(wg_ref, wu_ref, o_ref):
    o_ref[:, :FF_TILE] = wg_ref[...].astype(BF16)
    o_ref[:, FF_TILE:] = wu_ref[...].astype(BF16)


def _prep_gate_up(w_gate, w_up):
    d, d_ff = w_gate.shape
    nf = d_ff // FF_TILE
    col = pl.BlockSpec((d, FF_TILE), lambda f: (0, f))
    return pl.pallas_call(
        _pack_gate_up_kernel,
        grid=(nf,),
        in_specs=[col, col],
        out_specs=pl.BlockSpec((None, d, 2 * FF_TILE), lambda f: (f, 0, 0)),
        out_shape=jax.ShapeDtypeStruct((nf, d, 2 * FF_TILE), BF16),
        compiler_params=pltpu.CompilerParams(
            dimension_semantics=("parallel",), vmem_limit_bytes=VMEM_LIMIT),
        name="pack_gate_up",
    )(w_gate, w_up)


def _in_proj_kernel(x_ref, g_ref, w_ref, o_ref, h_ref):
    @pl.when(pl.program_id(1) == 0)
    def _():
        _rmsnorm_rows(x_ref, g_ref, h_ref)

    o_ref[...] = _dot(h_ref[...], w_ref[...])


def _in_proj(x, gain, w, *, tm=1024, tn=1024):
    t, d = x.shape
    n = w.shape[1]
    return pl.pallas_call(
        _in_proj_kernel,
        grid=(t // tm, n // tn),
        in_specs=[
            pl.BlockSpec((tm, d), lambda i, j: (i, 0), pipeline_mode=pl.Buffered(1)),
            pl.BlockSpec((1, d), lambda i, j: (0, 0)),
            pl.BlockSpec((d, tn), lambda i, j: (0, j)),
        ],
        out_specs=pl.BlockSpec((tm, tn), lambda i, j: (i, j)),
        out_shape=jax.ShapeDtypeStruct((t, n), F32),
        scratch_shapes=[pltpu.VMEM((tm, d), BF16)],
        compiler_params=pltpu.CompilerParams(
            dimension_semantics=("parallel", "arbitrary"),
            vmem_limit_bytes=VMEM_LIMIT),
        name="in_proj",
    )(x, gain.reshape(1, d), w)


def _out_proj_kernel(oh_ref, om_ref, wh_ref, wm_ref, x_ref, o_ref):
    o_ref[...] = x_ref[...] + _dot(oh_ref[...], wh_ref[...]) + _dot(om_ref[...], wm_ref[...])


def _out_proj(o_hgrn, o_moba, w, x, *, tm=1024, tn=1024):
    t, d = x.shape
    kh = o_hgrn.shape[1]
    return pl.pallas_call(
        _out_proj_kernel,
        grid=(t // tm, d // tn),
        in_specs=[
            pl.BlockSpec((tm, kh), lambda i, j: (i, 0)),
            pl.BlockSpec((tm, kh), lambda i, j: (i, 0)),
            pl.BlockSpec((kh, tn), lambda i, j: (0, j)),
            pl.BlockSpec((kh, tn), lambda i, j: (1, j)),
            pl.BlockSpec((tm, tn), lambda i, j: (i, j)),
        ],
        out_specs=pl.BlockSpec((tm, tn), lambda i, j: (i, j)),
        out_shape=jax.ShapeDtypeStruct((t, d), F32),
        compiler_params=pltpu.CompilerParams(
            dimension_semantics=("parallel", "arbitrary"),
            vmem_limit_bytes=VMEM_LIMIT),
        name="out_proj",
    )(o_hgrn, o_moba, w, w, x)


def _hgrn_kernel(hq_ref, hf_ref, hi_ref, hg_ref, lb_ref, gain_ref, o_ref,
                 st_ref, cum_ref, key_ref):
    c, sub, half = HGRN_CHUNK, HGRN_SUB, HGRN_SUB // 2
    rows_total = hq_ref.shape[0]
    lb_all = lb_ref[...]
    gain = gain_ref[...]
    tri = (lax.broadcasted_iota(jnp.int32, (c, c), 0)
           >= lax.broadcasted_iota(jnp.int32, (c, c), 1)).astype(BF16)
    lane = lax.broadcasted_iota(jnp.int32, (half, c), 1)
    trow = lax.broadcasted_iota(jnp.int32, (half, HEAD), 0)

    @pl.when(pl.program_id(2) == 0)
    def _():
        st_ref[...] = jnp.zeros_like(st_ref)

    def chunk(ci, carry):
        sl = pl.ds(pl.multiple_of(ci * c, c), c)
        forget_all = lb_all + (1.0 - lb_all) * jax.nn.sigmoid(hf_ref[sl, :])
        logf = jnp.log2(forget_all)
        hi = logf.astype(BF16)
        rest = logf - hi.astype(F32)
        mid = rest.astype(BF16)
        lo = (rest - mid.astype(F32)).astype(BF16)
        cum_all = _dot(tri, hi) + _dot(tri, mid) + _dot(tri, lo)
        key_all = cum_all - jnp.log2(1.0 - forget_all)

        for g in range(HGRN_GROUP):
            cols = slice(g * HEAD, (g + 1) * HEAD)
            cum = cum_all[:, cols]
            key = key_all[:, cols]
            hq = hq_ref[sl, cols]
            q = (hq * jax.nn.sigmoid(hq)) * (HEAD ** -0.5)
            vb = hi_ref[sl, cols].astype(BF16)
            cum_ref[g] = cum
            key_ref[g] = key
            last = cum[c - 1:c, :]

            st = st_ref[g]
            o = _dot_nt((q * jnp.exp2(cum)).astype(BF16), st.astype(BF16))
            k_end = jnp.exp2(last - key).astype(BF16)
            st_ref[g] = st * jnp.exp2(last) + _dot_tn(vb, k_end)

            blocks = []
            for i in range(c // sub):
                lo_row = i * sub
                off = None
                if i > 0:
                    edge = cum_ref[g, lo_row - 1:lo_row, :]
                    q_t = (q[lo_row:lo_row + sub, :]
                           * jnp.exp2(cum[lo_row:lo_row + sub, :] - edge)).astype(BF16)
                    k_t = jnp.exp2(edge - key[:lo_row, :]).astype(BF16)
                    k_t = jnp.concatenate([k_t, jnp.zeros((c - lo_row, HEAD), BF16)], axis=0)
                    off = _dot_nt(q_t, k_t)
                for h in range(2):
                    top = lo_row + h * half
                    cum_b = cum[top:top + half, :]
                    q_b = q[top:top + half, :]
                    a = jnp.zeros((half, c), F32)
                    for s in range((h + 1) * half):
                        r = lo_row + s
                        diff = cum_b - key_ref[g, r:r + 1, :]
                        if s > h * half:
                            diff = jnp.where(trow >= s - h * half, diff, -jnp.inf)
                        p = q_b * jnp.exp2(diff)
                        a = jnp.where(lane == r, jnp.sum(p, axis=-1, keepdims=True), a)
                    if off is not None:
                        a = a + off[h * half:(h + 1) * half, :]
                    blocks.append(a)
            scores = jnp.concatenate(blocks, axis=0)
            o = o + _dot(scores.astype(BF16), vb)

            r = lax.rsqrt(jnp.mean(o * o, axis=-1, keepdims=True) + RMS_EPS)
            hg = hg_ref[sl, cols]
            o_ref[sl, cols] = (((o * r) * gain) * (hg * jax.nn.sigmoid(hg))).astype(o_ref.dtype)
        return carry

    lax.fori_loop(0, rows_total // c, chunk, 0)


def _hgrn(proj, lower_bound, norm_gain):
    b, s, _ = proj.shape
    grp = HGRN_GROUP
    ng = N_HEADS // grp
    sb = min(HGRN_SEQ_BLOCK, s)
    col = lambda off: pl.BlockSpec((None, sb, grp * HEAD), lambda bi, gi, si: (bi, si, off + gi))
    return pl.pallas_call(
        _hgrn_kernel,
        grid=(b, ng, s // sb),
        in_specs=[
            col(0), col(ng), col(2 * ng), col(3 * ng),
            pl.BlockSpec((None, 1, grp * HEAD), lambda bi, gi, si: (gi, 0, 0)),
            pl.BlockSpec((1, HEAD), lambda bi, gi, si: (0, 0)),
        ],
        out_specs=pl.BlockSpec((None, sb, grp * HEAD), lambda bi, gi, si: (bi, si, gi)),
        out_shape=jax.ShapeDtypeStruct((b, s, MIX_WIDTH), BF16),
        scratch_shapes=[
            pltpu.VMEM((grp, HEAD, HEAD), F32),
            pltpu.VMEM((grp, HGRN_CHUNK, HEAD), F32),
            pltpu.VMEM((grp, HGRN_CHUNK, HEAD), F32),
        ],
        compiler_params=pltpu.CompilerParams(
            dimension_semantics=("parallel", "parallel", "arbitrary"),
            vmem_limit_bytes=VMEM_LIMIT),
        name="hgrn2",
    )(proj, proj, proj, proj, lower_bound.reshape(ng, 1, grp * HEAD), norm_gain.reshape(1, HEAD))


def _split3(v):
    hi = v.astype(BF16).astype(F32)
    mid = (v - hi).astype(BF16).astype(F32)
    lo = ((v - hi) - mid).astype(BF16).astype(F32)
    return hi, mid, lo


def _moba_kernel(slopes_ref, units_ref, q_ref, k_ref, v_ref, o_ref, kb_ref, vt_ref, kmean_ref):
    blk = MOBA_BLOCK
    seq = q_ref.shape[0]
    nb = seq // blk
    topk = min(MOBA_TOPK, max(nb - 1, 1))
    head = pl.program_id(1)
    to_log2 = (HEAD ** -0.5) * LOG2E
    slope2 = slopes_ref[head] * LOG2E
    unit = units_ref[head]
    neg_inf = jnp.full((1, blk), -jnp.inf, F32)

    k_pos = lax.broadcasted_iota(jnp.int32, (blk, HEAD), 0).astype(F32) * unit
    k_lane = lax.broadcasted_iota(jnp.int32, (blk, HEAD), 1)
    hi, mid, lo = _split3(k_pos)
    aug_k = jnp.where(k_lane == 0, hi, jnp.where(k_lane == 1, mid, jnp.where(
        k_lane == 2, lo, jnp.where(k_lane < 6, 1.0, 0.0)))).astype(BF16)
    q_pos = lax.broadcasted_iota(jnp.int32, (HEAD, blk), 1).astype(F32) * (-unit)
    q_row = lax.broadcasted_iota(jnp.int32, (HEAD, blk), 0)
    hi, mid, lo = _split3(q_pos)
    aug_q = jnp.where(q_row < 3, 1.0, jnp.where(q_row == 3, hi, jnp.where(
        q_row == 4, mid, jnp.where(q_row == 5, lo, 0.0)))).astype(BF16)

    kmean_ref[...] = jnp.zeros_like(kmean_ref)
    for n in range(nb):
        rows = slice(n * blk, (n + 1) * blk)
        kblk = k_ref[rows, :]
        kmean_ref[n:n + 1, :] = jnp.mean(kblk, axis=0, keepdims=True)
        kb_ref[rows, :HEAD] = kblk.astype(BF16)
        kb_ref[rows, HEAD:] = aug_k
        vt_ref[:, rows] = v_ref[rows, :].T.astype(BF16)

    causal = (lax.broadcasted_iota(jnp.int32, (blk, blk), 0)
              <= lax.broadcasted_iota(jnp.int32, (blk, blk), 1))

    for qb in range(nb):
        qsl = slice(qb * blk, (qb + 1) * blk)
        qt = q_ref[qsl, :].T
        if qb <= topk:
            picked = [None] * qb
        else:
            rows_p = -(-qb // 8) * 8
            blk_row = lax.broadcasted_iota(jnp.int32, (rows_p, blk), 0)
            gate = _dot(kmean_ref[0:rows_p, :], qt, precision=lax.Precision.HIGHEST)
            gate = jnp.where(blk_row < qb, gate, -jnp.inf)
            sel = jnp.zeros((rows_p, blk), F32)
            for _ in range(topk):
                best = jnp.max(gate, axis=0, keepdims=True)
                first = jnp.min(jnp.where(gate == best, blk_row, rows_p), axis=0, keepdims=True)
                pick = blk_row == first
                sel = jnp.where(pick, 1.0, sel)
                gate = jnp.where(pick, -jnp.inf, gate)
            picked = [sel[n:n + 1, :] > 0.0 for n in range(qb)]

        nk = (qb + 1) * blk
        qt_aug = jnp.concatenate([qt.astype(BF16), aug_q], axis=0)
        raw = _dot(kb_ref[0:nk, :], qt_aug)
        tiles, shifts = [], []
        for n in range(qb):
            tiles.append(raw[n * blk:(n + 1) * blk, :] * to_log2)
            offset = slope2 * float((n - qb) * blk)
            shifts.append(offset if picked[n] is None else jnp.where(picked[n], offset, neg_inf))
        tiles.append(jnp.where(causal, raw[qb * blk:nk, :] * to_log2, -jnp.inf))
        shifts.append(0.0)

        m = None
        for t, sh in zip(tiles, shifts):
            top = jnp.max(t, axis=0, keepdims=True) + sh
            m = top if m is None else jnp.maximum(m, top)
        probs, denom = [], None
        for t, sh in zip(tiles, shifts):
            p = jnp.exp2(t + (sh - m))
            probs.append(p.astype(BF16))
            part = jnp.sum(p, axis=0, keepdims=True)
            denom = part if denom is None else denom + part
        acc = _dot(vt_ref[:, 0:nk], jnp.concatenate(probs, axis=0))
        o_ref[qsl, :] = (acc / denom).T.astype(o_ref.dtype)


def _moba(proj):
    b, s, _ = proj.shape
    assert s % MOBA_BLOCK == 0
    nb = s // MOBA_BLOCK
    nbp = -(-nb // 8) * 8
    slopes = jnp.exp2(-8.0 * jnp.arange(1, N_HEADS + 1, dtype=F32) / N_HEADS)
    col = lambda off: pl.BlockSpec((None, s, HEAD), lambda bi, hi: (bi, 0, off + hi))
    return pl.pallas_call(
        _moba_kernel,
        grid=(b, N_HEADS),
        in_specs=[
            pl.BlockSpec(memory_space=pltpu.SMEM),
            pl.BlockSpec(memory_space=pltpu.SMEM),
            col(4 * N_HEADS), col(5 * N_HEADS), col(6 * N_HEADS),
        ],
        out_specs=pl.BlockSpec((None, s, HEAD), lambda bi, hi: (bi, 0, hi)),
        out_shape=jax.ShapeDtypeStruct((b, s, MIX_WIDTH), BF16),
        scratch_shapes=[
            pltpu.VMEM((s, 2 * HEAD), BF16),
            pltpu.VMEM((HEAD, s), BF16),
            pltpu.VMEM((nbp, HEAD), F32),
        ],
        compiler_params=pltpu.CompilerParams(
            dimension_semantics=("parallel", "parallel"),
            vmem_limit_bytes=VMEM_LIMIT),
        name="moba",
    )(slopes, slopes * HEAD ** 0.5, proj, proj, proj)


def kernel(x, ffn1_norm, ffn1_w_gate, ffn1_w_up, ffn1_w_down, mix_norm, w_in, hgrn_lower_bounds, hgrn_out_norm, w_out, ffn2_norm, ffn2_w_gate, ffn2_w_up, ffn2_w_down, final_norm):
    b, s, d = x.shape
    depth = ffn1_norm.shape[0]
    lb_all = jnp.cumsum(jax.nn.softmax(hgrn_lower_bounds.astype(F32), axis=0), axis=0)
    xs = x.reshape(b * s, d)
    for layer in range(depth):
        xs = _ffn(xs, ffn1_norm[layer], _prep_gate_up(ffn1_w_gate[layer], ffn1_w_up[layer]),
                  ffn1_w_down[layer].astype(BF16))
        proj = _in_proj(xs, mix_norm[layer], w_in[layer].astype(BF16)).reshape(b, s, -1)
        o_hgrn = _hgrn(proj, lb_all[layer], hgrn_out_norm[layer]).reshape(b * s, MIX_WIDTH)
        o_moba = _moba(proj).reshape(b * s, MIX_WIDTH)
        xs = _out_proj(o_hgrn, o_moba, w_out[layer].astype(BF16), xs)
        last = layer == depth - 1
        xs = _ffn(xs, ffn2_norm[layer], _prep_gate_up(ffn2_w_gate[layer], ffn2_w_up[layer]),
                  ffn2_w_down[layer].astype(BF16), final_norm if last else None)
    if depth == 0:
        raise NotImplementedError("depth 0 has no FFN to carry the final norm")
    return xs.reshape(b, s, d)
```

```python
import functools

import jax
import jax.numpy as jnp
from jax import lax
from jax.experimental import pallas as pl
from jax.experimental.pallas import tpu as pltpu

F32 = jnp.float32
BF16 = jnp.bfloat16

HEAD = 128
N_HEADS = 16
MIX_WIDTH = N_HEADS * HEAD
MOBA_BLOCK = 256
MOBA_TOPK = 3
RMS_EPS = 1e-6
MACARON_WEIGHT = 0.5
LOG2E = 1.4426950408889634

HGRN_CHUNK = 64
HGRN_SUB = 16
HGRN_GROUP = 4
HGRN_SEQ_BLOCK = 1024
HGRN_UNROLL = 2
FF_TILE = 256
NORM_ROWS = 16
NORM_UNROLL = 4

V7X_VMEM_BYTES = 64 * 1024 * 1024
VMEM_LIMIT = V7X_VMEM_BYTES - 6 * 1024 * 1024


def _dot(a, b, precision=None):
    return jnp.dot(a, b, preferred_element_type=F32, precision=precision)


def _dot_nt(a, b, precision=None):
    return lax.dot_general(a, b, (((1,), (1,)), ((), ())),
                           preferred_element_type=F32, precision=precision)


def _dot_tn(a, b):
    return lax.dot_general(a, b, (((0,), (0,)), ((), ())), preferred_element_type=F32)


def _rmsnorm_rows(x_ref, g_ref, h_ref, copy_ref=None):
    rows = x_ref.shape[0]
    gain = g_ref[...]

    def body(i, carry):
        sl = pl.ds(pl.multiple_of(i * NORM_ROWS, NORM_ROWS), NORM_ROWS)
        x = x_ref[sl, :]
        r = lax.rsqrt(jnp.mean(x * x, axis=-1, keepdims=True) + RMS_EPS)
        h_ref[sl, :] = ((x * r) * gain).astype(BF16)
        if copy_ref is not None:
            copy_ref[sl, :] = x
        return carry

    lax.fori_loop(0, rows // NORM_ROWS, body, 0, unroll=NORM_UNROLL)


def _ffn_kernel(*refs, final_norm):
    if final_norm:
        x_ref, g_ref, wgu_ref, wd_ref, gf_ref, o_ref, h_ref = refs
    else:
        x_ref, g_ref, wgu_ref, wd_ref, o_ref, h_ref = refs
    f = pl.program_id(1)

    @pl.when(f == 0)
    def _():
        _rmsnorm_rows(x_ref, g_ref, h_ref, copy_ref=o_ref)

    gu = _dot(h_ref[...], wgu_ref[...])
    gate = gu[:, :FF_TILE]
    up = gu[:, FF_TILE:]
    act = (gate * jax.nn.sigmoid(gate)) * up * MACARON_WEIGHT
    o_ref[...] += _dot(act.astype(BF16), wd_ref[...])

    if final_norm:
        @pl.when(f == pl.num_programs(1) - 1)
        def _():
            gain = gf_ref[...]

            def body(i, carry):
                sl = pl.ds(pl.multiple_of(i * NORM_ROWS, NORM_ROWS), NORM_ROWS)
                y = o_ref[sl, :]
                r = lax.rsqrt(jnp.mean(y * y, axis=-1, keepdims=True) + RMS_EPS)
                o_ref[sl, :] = (y * r) * gain
                return carry

            lax.fori_loop(0, o_ref.shape[0] // NORM_ROWS, body, 0, unroll=NORM_UNROLL)


def _ffn(x, gain, wgu, wd, final_gain=None, *, tm=512):
    t, d = x.shape
    nf = wgu.shape[0]
    final_norm = final_gain is not None
    in_specs = [
        pl.BlockSpec((tm, d), lambda i, f: (i, 0)),
        pl.BlockSpec((1, d), lambda i, f: (0, 0)),
        pl.BlockSpec((None, d, 2 * FF_TILE), lambda i, f: (f, 0, 0)),
        pl.BlockSpec((FF_TILE, d), lambda i, f: (f, 0)),
    ]
    args = [x, gain.reshape(1, d), wgu, wd]
    if final_norm:
        in_specs.append(pl.BlockSpec((1, d), lambda i, f: (0, 0)))
        args.append(final_gain.reshape(1, d))
    return pl.pallas_call(
        functools.partial(_ffn_kernel, final_norm=final_norm),
        grid=(t // tm, nf),
        in_specs=in_specs,
        out_specs=pl.BlockSpec((tm, d), lambda i, f: (i, 0)),
        out_shape=jax.ShapeDtypeStruct((t, d), F32),
        scratch_shapes=[pltpu.VMEM((tm, d), BF16)],
        compiler_params=pltpu.CompilerParams(
            dimension_semantics=("parallel", "arbitrary"),
            vmem_limit_bytes=VMEM_LIMIT),
        name="ffn_final" if final_norm else "ffn",
    )(*args)


def _pack_gate_up_kernel(wg_ref, wu_ref, o_ref):
    o_ref[:, :FF_TILE] = wg_ref[...].astype(BF16)
    o_ref[:, FF_TILE:] = wu_ref[...].astype(BF16)


def _prep_gate_up(w_gate, w_up):
    d, d_ff = w_gate.shape
    nf = d_ff // FF_TILE
    col = pl.BlockSpec((d, FF_TILE), lambda f: (0, f))
    return pl.pallas_call(
        _pack_gate_up_kernel,
        grid=(nf,),
        in_specs=[col, col],
        out_specs=pl.BlockSpec((None, d, 2 * FF_TILE), lambda f: (f, 0, 0)),
        out_shape=jax.ShapeDtypeStruct((nf, d, 2 * FF_TILE), BF16),
        compiler_params=pltpu.CompilerParams(
            dimension_semantics=("parallel",), vmem_limit_bytes=VMEM_LIMIT),
        name="pack_gate_up",
    )(w_gate, w_up)


def _in_proj_kernel(x_ref, g_ref, w_ref, o_ref, h_ref):
    @pl.when(pl.program_id(1) == 0)
    def _():
        _rmsnorm_rows(x_ref, g_ref, h_ref)

    o_ref[...] = _dot(h_ref[...], w_ref[...])


def _in_proj(x, gain, w, *, tm=1024, tn=1024):
    t, d = x.shape
    n = w.shape[1]
    return pl.pallas_call(
        _in_proj_kernel,
        grid=(t // tm, n // tn),
        in_specs=[
            pl.BlockSpec((tm, d), lambda i, j: (i, 0), pipeline_mode=pl.Buffered(1)),
            pl.BlockSpec((1, d), lambda i, j: (0, 0)),
            pl.BlockSpec((d, tn), lambda i, j: (0, j)),
        ],
        out_specs=pl.BlockSpec((tm, tn), lambda i, j: (i, j)),
        out_shape=jax.ShapeDtypeStruct((t, n), F32),
        scratch_shapes=[pltpu.VMEM((tm, d), BF16)],
        compiler_params=pltpu.CompilerParams(
            dimension_semantics=("parallel", "arbitrary"),
            vmem_limit_bytes=VMEM_LIMIT),
        name="in_proj",
    )(x, gain.reshape(1, d), w)


def _out_proj_kernel(oh_ref, om_ref, wh_ref, wm_ref, x_ref, o_ref):
    o_ref[...] = x_ref[...] + _dot(oh_ref[...], wh_ref[...]) + _dot(om_ref[...], wm_ref[...])


def _out_proj(o_hgrn, o_moba, w, x, *, tm=1024, tn=1024):
    t, d = x.shape
    kh = o_hgrn.shape[1]
    return pl.pallas_call(
        _out_proj_kernel,
        grid=(t // tm, d // tn),
        in_specs=[
            pl.BlockSpec((tm, kh), lambda i, j: (i, 0)),
            pl.BlockSpec((tm, kh), lambda i, j: (i, 0)),
            pl.BlockSpec((kh, tn), lambda i, j: (0, j)),
            pl.BlockSpec((kh, tn), lambda i, j: (1, j)),
            pl.BlockSpec((tm, tn), lambda i, j: (i, j)),
        ],
        out_specs=pl.BlockSpec((tm, tn), lambda i, j: (i, j)),
        out_shape=jax.ShapeDtypeStruct((t, d), F32),
        compiler_params=pltpu.CompilerParams(
            dimension_semantics=("parallel", "arbitrary"),
            vmem_limit_bytes=VMEM_LIMIT),
        name="out_proj",
    )(o_hgrn, o_moba, w, w, x)


def _hgrn_kernel(hq_ref, hf_ref, hi_ref, hg_ref, lb_ref, gain_ref, o_ref,
                 st_ref, cum_ref, key_ref):
    c, sub, half = HGRN_CHUNK, HGRN_SUB, HGRN_SUB // 2
    rows_total = hq_ref.shape[0]
    lb_all = lb_ref[...]
    gain = gain_ref[...]
    tri = (lax.broadcasted_iota(jnp.int32, (c, c), 0)
           >= lax.broadcasted_iota(jnp.int32, (c, c), 1)).astype(BF16)
    lane = lax.broadcasted_iota(jnp.int32, (half, c), 1)
    trow = lax.broadcasted_iota(jnp.int32, (half, HEAD), 0)

    @pl.when(pl.program_id(2) == 0)
    def _():
        st_ref[...] = jnp.zeros_like(st_ref)

    def chunk(ci, carry):
        sl = pl.ds(pl.multiple_of(ci * c, c), c)
        forget_all = lb_all + (1.0 - lb_all) * jax.nn.sigmoid(hf_ref[sl, :])
        logf = jnp.log2(forget_all)
        hi = logf.astype(BF16)
        rest = logf - hi.astype(F32)
        mid = rest.astype(BF16)
        lo = (rest - mid.astype(F32)).astype(BF16)
        cum_all = _dot(tri, hi) + _dot(tri, mid) + _dot(tri, lo)
        key_all = cum_all - jnp.log2(1.0 - forget_all)

        heads = range(HGRN_GROUP)
        cols = [slice(g * HEAD, (g + 1) * HEAD) for g in heads]
        cum = [cum_all[:, cols[g]] for g in heads]
        key = [key_all[:, cols[g]] for g in heads]
        q, vb, o = [], [], []
        for g in heads:
            hq = hq_ref[sl, cols[g]]
            q.append((hq * jax.nn.sigmoid(hq)) * (HEAD ** -0.5))
            vb.append(hi_ref[sl, cols[g]].astype(BF16))
            cum_ref[g] = cum[g]
            key_ref[g] = key[g]
        for g in heads:
            last = cum[g][c - 1:c, :]
            st = st_ref[g]
            o.append(_dot_nt((q[g] * jnp.exp2(cum[g])).astype(BF16), st.astype(BF16)))
            k_end = jnp.exp2(last - key[g]).astype(BF16)
            st_ref[g] = st * jnp.exp2(last) + _dot_tn(vb[g], k_end)

        off = {}
        for i in range(1, c // sub):
            lo_row = i * sub
            for g in heads:
                edge = cum_ref[g, lo_row - 1:lo_row, :]
                q_t = (q[g][lo_row:lo_row + sub, :]
                       * jnp.exp2(cum[g][lo_row:lo_row + sub, :] - edge)).astype(BF16)
                k_t = jnp.exp2(edge - key[g][:lo_row, :]).astype(BF16)
                k_t = jnp.concatenate([k_t, jnp.zeros((c - lo_row, HEAD), BF16)], axis=0)
                off[g, i] = _dot_nt(q_t, k_t)

        blocks = [[] for _ in heads]
        for i in range(c // sub):
            lo_row = i * sub
            for h in range(2):
                top = lo_row + h * half
                for g in heads:
                    cum_b = cum[g][top:top + half, :]
                    q_b = q[g][top:top + half, :]
                    a = jnp.zeros((half, c), F32)
                    for s in range((h + 1) * half):
                        r = lo_row + s
                        diff = cum_b - key_ref[g, r:r + 1, :]
                        if s > h * half:
                            diff = jnp.where(trow >= s - h * half, diff, -jnp.inf)
                        p = q_b * jnp.exp2(diff)
                        a = jnp.where(lane == r, jnp.sum(p, axis=-1, keepdims=True), a)
                    if i > 0:
                        a = a + off[g, i][h * half:(h + 1) * half, :]
                    blocks[g].append(a)

        for g in heads:
            scores = jnp.concatenate(blocks[g], axis=0)
            og = o[g] + _dot(scores.astype(BF16), vb[g])
            r = lax.rsqrt(jnp.mean(og * og, axis=-1, keepdims=True) + RMS_EPS)
            hg = hg_ref[sl, cols[g]]
            o_ref[sl, cols[g]] = (((og * r) * gain) * (hg * jax.nn.sigmoid(hg))).astype(o_ref.dtype)
        return carry

    lax.fori_loop(0, rows_total // c, chunk, 0, unroll=HGRN_UNROLL)


def _hgrn(proj, lower_bound, norm_gain):
    b, s, _ = proj.shape
    grp = HGRN_GROUP
    ng = N_HEADS // grp
    sb = min(HGRN_SEQ_BLOCK, s)
    col = lambda off: pl.BlockSpec((None, sb, grp * HEAD), lambda bi, gi, si: (bi, si, off + gi))
    return pl.pallas_call(
        _hgrn_kernel,
        grid=(b, ng, s // sb),
        in_specs=[
            col(0), col(ng), col(2 * ng), col(3 * ng),
            pl.BlockSpec((None, 1, grp * HEAD), lambda bi, gi, si: (gi, 0, 0)),
            pl.BlockSpec((1, HEAD), lambda bi, gi, si: (0, 0)),
        ],
        out_specs=pl.BlockSpec((None, sb, grp * HEAD), lambda bi, gi, si: (bi, si, gi)),
        out_shape=jax.ShapeDtypeStruct((b, s, MIX_WIDTH), BF16),
        scratch_shapes=[
            pltpu.VMEM((grp, HEAD, HEAD), F32),
            pltpu.VMEM((grp, HGRN_CHUNK, HEAD), F32),
            pltpu.VMEM((grp, HGRN_CHUNK, HEAD), F32),
        ],
        compiler_params=pltpu.CompilerParams(
            dimension_semantics=("parallel", "parallel", "arbitrary"),
            vmem_limit_bytes=VMEM_LIMIT),
        name="hgrn2",
    )(proj, proj, proj, proj, lower_bound.reshape(ng, 1, grp * HEAD), norm_gain.reshape(1, HEAD))


def _split3(v):
    hi = v.astype(BF16).astype(F32)
    mid = (v - hi).astype(BF16).astype(F32)
    lo = ((v - hi) - mid).astype(BF16).astype(F32)
    return hi, mid, lo


def _moba_kernel(slopes_ref, units_ref, q_ref, k_ref, v_ref, o_ref, kb_ref, vt_ref, kmean_ref):
    blk = MOBA_BLOCK
    seq = q_ref.shape[0]
    nb = seq // blk
    topk = min(MOBA_TOPK, max(nb - 1, 1))
    head = pl.program_id(1)
    to_log2 = (HEAD ** -0.5) * LOG2E
    slope2 = slopes_ref[head] * LOG2E
    unit = units_ref[head]
    neg_inf = jnp.full((1, blk), -jnp.inf, F32)

    k_pos = lax.broadcasted_iota(jnp.int32, (blk, HEAD), 0).astype(F32) * unit
    k_lane = lax.broadcasted_iota(jnp.int32, (blk, HEAD), 1)
    hi, mid, lo = _split3(k_pos)
    aug_k = jnp.where(k_lane == 0, hi, jnp.where(k_lane == 1, mid, jnp.where(
        k_lane == 2, lo, jnp.where(k_lane < 6, 1.0, 0.0)))).astype(BF16)
    q_pos = lax.broadcasted_iota(jnp.int32, (HEAD, blk), 1).astype(F32) * (-unit)
    q_row = lax.broadcasted_iota(jnp.int32, (HEAD, blk), 0)
    hi, mid, lo = _split3(q_pos)
    aug_q = jnp.where(q_row < 3, 1.0, jnp.where(q_row == 3, hi, jnp.where(
        q_row == 4, mid, jnp.where(q_row == 5, lo, 0.0)))).astype(BF16)

    kmean_ref[...] = jnp.zeros_like(kmean_ref)
    for n in range(nb):
        rows = slice(n * blk, (n + 1) * blk)
        kblk = k_ref[rows, :]
        kmean_ref[n:n + 1, :] = jnp.mean(kblk, axis=0, keepdims=True)
        kb_ref[rows, :HEAD] = kblk.astype(BF16)
        kb_ref[rows, HEAD:] = aug_k
        vt_ref[:, rows] = v_ref[rows, :].T.astype(BF16)

    causal = (lax.broadcasted_iota(jnp.int32, (blk, blk), 0)
              <= lax.broadcasted_iota(jnp.int32, (blk, blk), 1))

    def score(qb):
        qsl = slice(qb * blk, (qb + 1) * blk)
        qt = q_ref[qsl, :].T
        if qb <= topk:
            picked = [None] * qb
        else:
            rows_p = -(-qb // 8) * 8
            blk_row = lax.broadcasted_iota(jnp.int32, (rows_p, blk), 0)
            gate = _dot(kmean_ref[0:rows_p, :], qt, precision=lax.Precision.HIGHEST)
            gate = jnp.where(blk_row < qb, gate, -jnp.inf)
            sel = jnp.zeros((rows_p, blk), F32)
            for _ in range(topk):
                best = jnp.max(gate, axis=0, keepdims=True)
                first = jnp.min(jnp.where(gate == best, blk_row, rows_p), axis=0, keepdims=True)
                pick = blk_row == first
                sel = jnp.where(pick, 1.0, sel)
                gate = jnp.where(pick, -jnp.inf, gate)
            picked = [sel[n:n + 1, :] > 0.0 for n in range(qb)]

        nk = (qb + 1) * blk
        qt_aug = jnp.concatenate([qt.astype(BF16), aug_q], axis=0)
        raw = _dot(kb_ref[0:nk, :], qt_aug)
        return raw, picked

    def softmax(qb, raw, picked):
        nk = (qb + 1) * blk
        tiles, shifts = [], []
        for n in range(qb):
            tiles.append(raw[n * blk:(n + 1) * blk, :] * to_log2)
            offset = slope2 * float((n - qb) * blk)
            shifts.append(offset if picked[n] is None else jnp.where(picked[n], offset, neg_inf))
        tiles.append(jnp.where(causal, raw[qb * blk:nk, :] * to_log2, -jnp.inf))
        shifts.append(0.0)

        m = None
        for t, sh in zip(tiles, shifts):
            top = jnp.max(t, axis=0, keepdims=True) + sh
            m = top if m is None else jnp.maximum(m, top)
        probs, denom = [], None
        for t, sh in zip(tiles, shifts):
            p = jnp.exp2(t + (sh - m))
            probs.append(p.astype(BF16))
            part = jnp.sum(p, axis=0, keepdims=True)
            denom = part if denom is None else denom + part
        return jnp.concatenate(probs, axis=0), denom

    def output(qb, probs, denom):
        nk = (qb + 1) * blk
        acc = _dot(vt_ref[:, 0:nk], probs)
        o_ref[qb * blk:nk, :] = (acc / denom).T.astype(o_ref.dtype)

    scored, soft = {}, {}
    for step in range(nb + 2):
        if step < nb:
            scored[step] = score(step)
        if 0 <= step - 1 < nb:
            soft[step - 1] = softmax(step - 1, *scored.pop(step - 1))
        if 0 <= step - 2 < nb:
            output(step - 2, *soft.pop(step - 2))


def _moba(proj):
    b, s, _ = proj.shape
    assert s % MOBA_BLOCK == 0
    nb = s // MOBA_BLOCK
    nbp = -(-nb // 8) * 8
    slopes = jnp.exp2(-8.0 * jnp.arange(1, N_HEADS + 1, dtype=F32) / N_HEADS)
    col = lambda off: pl.BlockSpec((None, s, HEAD), lambda bi, hi: (bi, 0, off + hi))
    return pl.pallas_call(
        _moba_kernel,
        grid=(b, N_HEADS),
        in_specs=[
            pl.BlockSpec(memory_space=pltpu.SMEM),
            pl.BlockSpec(memory_space=pltpu.SMEM),
            col(4 * N_HEADS), col(5 * N_HEADS), col(6 * N_HEADS),
        ],
        out_specs=pl.BlockSpec((None, s, HEAD), lambda bi, hi: (bi, 0, hi)),
        out_shape=jax.ShapeDtypeStruct((b, s, MIX_WIDTH), BF16),
        scratch_shapes=[
            pltpu.VMEM((s, 2 * HEAD), BF16),
            pltpu.VMEM((HEAD, s), BF16),
            pltpu.VMEM((nbp, HEAD), F32),
        ],
        compiler_params=pltpu.CompilerParams(
            dimension_semantics=("parallel", "parallel"),
            vmem_limit_bytes=VMEM_LIMIT),
        name="moba",
    )(slopes, slopes * HEAD ** 0.5, proj, proj, proj)


def kernel(x, ffn1_norm, ffn1_w_gate, ffn1_w_up, ffn1_w_down, mix_norm, w_in, hgrn_lower_bounds, hgrn_out_norm, w_out, ffn2_norm, ffn2_w_gate, ffn2_w_up, ffn2_w_down, final_norm):
    b, s, d = x.shape
    depth = ffn1_norm.shape[0]
    lb_all = jnp.cumsum(jax.nn.softmax(hgrn_lower_bounds.astype(F32), axis=0), axis=0)
    xs = x.reshape(b * s, d)
    for layer in range(depth):
        xs = _ffn(xs, ffn1_norm[layer], _prep_gate_up(ffn1_w_gate[layer], ffn1_w_up[layer]),
                  ffn1_w_down[layer].astype(BF16))
        proj = _in_proj(xs, mix_norm[layer], w_in[layer].astype(BF16)).reshape(b, s, -1)
        o_hgrn = _hgrn(proj, lb_all[layer], hgrn_out_norm[layer]).reshape(b * s, MIX_WIDTH)
        o_moba = _moba(proj).reshape(b * s, MIX_WIDTH)
        xs = _out_proj(o_hgrn, o_moba, w_out[layer].astype(BF16), xs)
        last = layer == depth - 1
        xs = _ffn(xs, ffn2_norm[layer], _prep_gate_up(ffn2_w_gate[layer], ffn2_w_up[layer]),
                  ffn2_w_down[layer].astype(BF16), final_norm if last else None)
    if depth == 0:
        raise NotImplementedError("depth 0 has no FFN to carry the final norm")
    return xs.reshape(b, s, d)
```

```python
import functools

import jax
import jax.numpy as jnp
from jax import lax
from jax.experimental import pallas as pl
from jax.experimental.pallas import tpu as pltpu

F32 = jnp.float32
BF16 = jnp.bfloat16

HEAD = 128
N_HEADS = 16
MIX_WIDTH = N_HEADS * HEAD
MOBA_BLOCK = 256
MOBA_TOPK = 3
RMS_EPS = 1e-6
MACARON_WEIGHT = 0.5
LOG2E = 1.4426950408889634

HGRN_CHUNK = 64
SUBLANES = 8
HGRN_SUB = 8
HGRN_GROUP = 8
HGRN_SEQ_BLOCK = 1024
HGRN_UNROLL = 2
FF_TILE = 256
FF_PAIR = 2
NORM_ROWS = 16
NORM_UNROLL = 4

V7X_VMEM_BYTES = 64 * 1024 * 1024
VMEM_LIMIT = V7X_VMEM_BYTES - 6 * 1024 * 1024
FFN_VMEM_LIMIT = V7X_VMEM_BYTES - 1 * 1024 * 1024


def _dot(a, b, precision=None):
    return jnp.dot(a, b, preferred_element_type=F32, precision=precision)


def _dot_nt(a, b, precision=None):
    return lax.dot_general(a, b, (((1,), (1,)), ((), ())),
                           preferred_element_type=F32, precision=precision)


def _dot_tn(a, b):
    return lax.dot_general(a, b, (((0,), (0,)), ((), ())), preferred_element_type=F32)


def _rmsnorm_rows(x_ref, g_ref, o_ref):
    groups = x_ref.shape[0] // NORM_ROWS
    gain = g_ref[...]

    def rows(i):
        return pl.ds(pl.multiple_of(i * NORM_ROWS, NORM_ROWS), NORM_ROWS)

    def inv_rms(i):
        x = x_ref[rows(i), :]
        return lax.rsqrt(jnp.mean(x * x, axis=-1, keepdims=True) + RMS_EPS)

    def body(i, r):
        r_next = inv_rms(jnp.minimum(i + 1, groups - 1))
        o_ref[rows(i), :] = ((x_ref[rows(i), :] * r) * gain).astype(o_ref.dtype)
        return r_next

    lax.fori_loop(0, groups, body, inv_rms(0), unroll=NORM_UNROLL)


def _ffn_kernel(*refs, final_norm, n_tiles):
    if final_norm:
        x_ref, g_ref, wgu_ref, wd_ref, gf_ref, o_ref, h_ref = refs
    else:
        x_ref, g_ref, wgu_ref, wd_ref, o_ref, h_ref = refs
    f = pl.program_id(1)
    last = pl.num_programs(1) - 1
    tail = n_tiles - last * FF_PAIR

    def tile(k):
        gu = _dot(h_ref[...], wgu_ref[k])
        gate = gu[:, :FF_TILE]
        up = gu[:, FF_TILE:]
        act = (gate * jax.nn.sigmoid(gate)) * up * MACARON_WEIGHT
        return _dot(act.astype(BF16), wd_ref[k * FF_TILE:(k + 1) * FF_TILE, :])

    @pl.when(f == 0)
    def _():
        _rmsnorm_rows(x_ref, g_ref, h_ref)
        o_ref[...] = x_ref[...] + tile(0)
        for k in range(1, FF_PAIR):
            o_ref[...] += tile(k)

    @pl.when(jnp.logical_and(f > 0, f < last))
    def _():
        for k in range(FF_PAIR):
            o_ref[...] += tile(k)

    @pl.when(f == last)
    def _():
        for k in range(tail):
            o_ref[...] += tile(k)
        if final_norm:
            _rmsnorm_rows(o_ref, gf_ref, o_ref)


def _ffn(x, gain, wgu, wd, final_gain=None, *, tm=512):
    t, d = x.shape
    nf = wgu.shape[0]
    steps = pl.cdiv(nf, FF_PAIR)
    assert steps >= 2
    final_norm = final_gain is not None
    in_specs = [
        pl.BlockSpec((tm, d), lambda i, f: (i, 0)),
        pl.BlockSpec((1, d), lambda i, f: (0, 0)),
        pl.BlockSpec((FF_PAIR, d, 2 * FF_TILE), lambda i, f: (f, 0, 0)),
        pl.BlockSpec((FF_PAIR * FF_TILE, d), lambda i, f: (f, 0)),
    ]
    args = [x, gain.reshape(1, d), wgu, wd]
    if final_norm:
        in_specs.append(pl.BlockSpec((1, d), lambda i, f: (0, 0)))
        args.append(final_gain.reshape(1, d))
    return pl.pallas_call(
        functools.partial(_ffn_kernel, final_norm=final_norm, n_tiles=nf),
        grid=(t // tm, steps),
        in_specs=in_specs,
        out_specs=pl.BlockSpec((tm, d), lambda i, f: (i, 0)),
        out_shape=jax.ShapeDtypeStruct((t, d), F32),
        scratch_shapes=[pltpu.VMEM((tm, d), BF16)],
        compiler_params=pltpu.CompilerParams(
            dimension_semantics=("parallel", "arbitrary"),
            vmem_limit_bytes=FFN_VMEM_LIMIT),
        name="ffn_final" if final_norm else "ffn",
    )(*args)


def _pack_gate_up_kernel(wg_ref, wu_ref, o_ref):
    o_ref[:, :FF_TILE] = wg_ref[...].astype(BF16)
    o_ref[:, FF_TILE:] = wu_ref[...].astype(BF16)


def _prep_gate_up(w_gate, w_up):
    d, d_ff = w_gate.shape
    nf = d_ff // FF_TILE
    col = pl.BlockSpec((d, FF_TILE), lambda f: (0, f))
    return pl.pallas_call(
        _pack_gate_up_kernel,
        grid=(nf,),
        in_specs=[col, col],
        out_specs=pl.BlockSpec((None, d, 2 * FF_TILE), lambda f: (f, 0, 0)),
        out_shape=jax.ShapeDtypeStruct((nf, d, 2 * FF_TILE), BF16),
        compiler_params=pltpu.CompilerParams(
            dimension_semantics=("parallel",), vmem_limit_bytes=VMEM_LIMIT),
        name="pack_gate_up",
    )(w_gate, w_up)


def _in_proj_kernel(x_hbm, g_ref, w_ref, lb_ref, o_ref, x_ref, h_ref, sem, *, tiles_per_part):
    i = pl.program_id(0)
    j = pl.program_id(1)
    part = j // tiles_per_part
    tm = x_ref.shape[0]

    def fetch(tile):
        return pltpu.make_async_copy(x_hbm.at[pl.ds(tile * tm, tm), :], x_ref, sem)

    @pl.when(j == 0)
    def _():
        @pl.when(i == 0)
        def _():
            fetch(0).start()

        fetch(i).wait()
        _rmsnorm_rows(x_ref, g_ref, h_ref)

    @pl.when(jnp.logical_and(j == 1, i + 1 < pl.num_programs(0)))
    def _():
        fetch(i + 1).start()

    @pl.when(jnp.logical_or(part == 0, part == 3))
    def _():
        z = _dot(h_ref[...], w_ref[...])
        o_ref[...] = (z * jax.nn.sigmoid(z)) * jnp.where(part == 0, HEAD ** -0.5, 1.0)

    @pl.when(part == 1)
    def _():
        lb = lb_ref[...]
        o_ref[...] = lb + (1.0 - lb) * jax.nn.sigmoid(_dot(h_ref[...], w_ref[...]))

    @pl.when(jnp.logical_or(part == 2, part > 3))
    def _():
        o_ref[...] = _dot(h_ref[...], w_ref[...])


def _in_proj(x, gain, w, lower_bound, *, tm=1024, tn=1024):
    t, d = x.shape
    n = w.shape[1]
    tiles_per_part = MIX_WIDTH // tn
    assert n // tn >= 2 and t % tm == 0
    return pl.pallas_call(
        functools.partial(_in_proj_kernel, tiles_per_part=tiles_per_part),
        grid=(t // tm, n // tn),
        in_specs=[
            pl.BlockSpec(memory_space=pl.ANY),
            pl.BlockSpec((1, d), lambda i, j: (0, 0)),
            pl.BlockSpec((d, tn), lambda i, j: (0, j)),
            pl.BlockSpec((1, tn),
                         lambda i, j: (0, jnp.clip(j - tiles_per_part, 0, tiles_per_part - 1))),
        ],
        out_specs=pl.BlockSpec((tm, tn), lambda i, j: (i, j)),
        out_shape=jax.ShapeDtypeStruct((t, n), F32),
        scratch_shapes=[
            pltpu.VMEM((tm, d), F32),
            pltpu.VMEM((tm, d), BF16),
            pltpu.SemaphoreType.DMA(()),
        ],
        compiler_params=pltpu.CompilerParams(
            dimension_semantics=("arbitrary", "arbitrary"),
            vmem_limit_bytes=VMEM_LIMIT),
        name="in_proj",
    )(x, gain.reshape(1, d), w, lower_bound.reshape(1, MIX_WIDTH))


def _out_proj_kernel(oh_ref, om_ref, wh_ref, wm_ref, x_ref, o_ref):
    o_ref[...] = x_ref[...] + _dot(oh_ref[...], wh_ref[...]) + _dot(om_ref[...], wm_ref[...])


def _out_proj(o_hgrn, o_moba, w, x, *, tm=1024, tn=1024):
    t, d = x.shape
    kh = o_hgrn.shape[1]
    return pl.pallas_call(
        _out_proj_kernel,
        grid=(t // tm, d // tn),
        in_specs=[
            pl.BlockSpec((tm, kh), lambda i, j: (i, 0)),
            pl.BlockSpec((tm, kh), lambda i, j: (i, 0)),
            pl.BlockSpec((kh, tn), lambda i, j: (0, j)),
            pl.BlockSpec((kh, tn), lambda i, j: (1, j)),
            pl.BlockSpec((tm, tn), lambda i, j: (i, j)),
        ],
        out_specs=pl.BlockSpec((tm, tn), lambda i, j: (i, j)),
        out_shape=jax.ShapeDtypeStruct((t, d), F32),
        compiler_params=pltpu.CompilerParams(
            dimension_semantics=("parallel", "arbitrary"),
            vmem_limit_bytes=VMEM_LIMIT),
        name="out_proj",
    )(o_hgrn, o_moba, w, w, x)


def _hgrn_kernel(q_ref, f_ref, v_ref, gate_ref, gain_ref, o_ref,
                 st_ref, cum_ref, key_ref):
    c, sub, half = HGRN_CHUNK, HGRN_SUB, SUBLANES
    rows_total = q_ref.shape[0]
    gain = gain_ref[...]
    tri = (lax.broadcasted_iota(jnp.int32, (c, c), 0)
           >= lax.broadcasted_iota(jnp.int32, (c, c), 1)).astype(BF16)
    lane = lax.broadcasted_iota(jnp.int32, (half, c), 1)
    trow = lax.broadcasted_iota(jnp.int32, (half, HEAD), 0)

    @pl.when(pl.program_id(2) == 0)
    def _():
        st_ref[...] = jnp.zeros_like(st_ref)

    def chunk(ci, carry):
        sl = pl.ds(pl.multiple_of(ci * c, c), c)
        forget_all = f_ref[sl, :]
        logf = jnp.log2(forget_all)
        hi = logf.astype(BF16)
        rest = logf - hi.astype(F32)
        mid = rest.astype(BF16)
        lo = (rest - mid.astype(F32)).astype(BF16)
        cum_all = _dot(tri, hi) + _dot(tri, mid) + _dot(tri, lo)
        key_all = cum_all - jnp.log2(1.0 - forget_all)

        heads = range(HGRN_GROUP)
        cols = [slice(g * HEAD, (g + 1) * HEAD) for g in heads]
        cum = [cum_all[:, cols[g]] for g in heads]
        key = [key_all[:, cols[g]] for g in heads]
        q, vb, o = [], [], []
        for g in heads:
            q.append(q_ref[sl, cols[g]])
            vb.append(v_ref[sl, cols[g]].astype(BF16))
            cum_ref[g] = cum[g]
            key_ref[g] = key[g]
        for g in heads:
            last = cum[g][c - 1:c, :]
            st = st_ref[g]
            o.append(_dot_nt((q[g] * jnp.exp2(cum[g])).astype(BF16), st.astype(BF16)))
            k_end = jnp.exp2(last - key[g]).astype(BF16)
            st_ref[g] = st * jnp.exp2(last) + _dot_tn(vb[g], k_end)

        off = {}
        for i in range(1, c // sub):
            lo_row = i * sub
            for g in heads:
                edge = cum_ref[g, lo_row - 1:lo_row, :]
                q_t = (q[g][lo_row:lo_row + sub, :]
                       * jnp.exp2(cum[g][lo_row:lo_row + sub, :] - edge)).astype(BF16)
                k_t = jnp.exp2(edge - key[g][:lo_row, :])
                k_t = jnp.concatenate([k_t, jnp.zeros((c - lo_row, HEAD), F32)], axis=0)
                off[g, i] = _dot_nt(q_t, k_t.astype(BF16))

        blocks = [[] for _ in heads]
        for i in range(c // sub):
            lo_row = i * sub
            for h in range(sub // half):
                top = lo_row + h * half
                for g in heads:
                    cum_b = cum[g][top:top + half, :]
                    q_b = q[g][top:top + half, :]
                    a = jnp.zeros((half, c), F32)
                    for s in range((h + 1) * half):
                        r = lo_row + s
                        diff = cum_b - key_ref[g, r:r + 1, :]
                        if s > h * half:
                            diff = jnp.where(trow >= s - h * half, diff, -jnp.inf)
                        p = q_b * jnp.exp2(diff)
                        a = jnp.where(lane == r, jnp.sum(p, axis=-1, keepdims=True), a)
                    if i > 0:
                        a = a + off[g, i][h * half:(h + 1) * half, :]
                    blocks[g].append(a)

        for g in heads:
            scores = jnp.concatenate(blocks[g], axis=0)
            og = o[g] + _dot(scores.astype(BF16), vb[g])
            r = lax.rsqrt(jnp.mean(og * og, axis=-1, keepdims=True) + RMS_EPS)
            o_ref[sl, cols[g]] = (((og * r) * gain) * gate_ref[sl, cols[g]]).astype(o_ref.dtype)
        return carry

    lax.fori_loop(0, rows_total // c, chunk, 0, unroll=HGRN_UNROLL)


def _hgrn(proj, norm_gain):
    b, s, _ = proj.shape
    grp = HGRN_GROUP
    ng = N_HEADS // grp
    sb = min(HGRN_SEQ_BLOCK, s)
    assert s % sb == 0 and sb % (HGRN_CHUNK * HGRN_UNROLL) == 0
    col = lambda off: pl.BlockSpec((None, sb, grp * HEAD), lambda bi, gi, si: (bi, si, off + gi))
    return pl.pallas_call(
        _hgrn_kernel,
        grid=(b, ng, s // sb),
        in_specs=[
            col(0), col(ng), col(2 * ng), col(3 * ng),
            pl.BlockSpec((1, HEAD), lambda bi, gi, si: (0, 0)),
        ],
        out_specs=pl.BlockSpec((None, sb, grp * HEAD), lambda bi, gi, si: (bi, si, gi)),
        out_shape=jax.ShapeDtypeStruct((b, s, MIX_WIDTH), BF16),
        scratch_shapes=[
            pltpu.VMEM((grp, HEAD, HEAD), F32),
            pltpu.VMEM((grp, HGRN_CHUNK, HEAD), F32),
            pltpu.VMEM((grp, HGRN_CHUNK, HEAD), F32),
        ],
        compiler_params=pltpu.CompilerParams(
            dimension_semantics=("parallel", "parallel", "arbitrary"),
            vmem_limit_bytes=VMEM_LIMIT),
        name="hgrn2",
    )(proj, proj, proj, proj, norm_gain.reshape(1, HEAD))


def _split3(v):
    hi = v.astype(BF16).astype(F32)
    mid = (v - hi).astype(BF16).astype(F32)
    lo = ((v - hi) - mid).astype(BF16).astype(F32)
    return hi, mid, lo


def _moba_kernel(slopes_ref, units_ref, q_ref, k_ref, v_ref, o_ref, kb_ref, vt_ref, kmean_ref):
    blk = MOBA_BLOCK
    seq = q_ref.shape[0]
    nb = seq // blk
    topk = min(MOBA_TOPK, max(nb - 1, 1))
    head = pl.program_id(1)
    to_log2 = (HEAD ** -0.5) * LOG2E
    slope2 = slopes_ref[head] * LOG2E
    unit = units_ref[head]
    neg_inf = jnp.full((1, blk), -jnp.inf, F32)

    k_pos = lax.broadcasted_iota(jnp.int32, (blk, HEAD), 0).astype(F32) * unit
    k_lane = lax.broadcasted_iota(jnp.int32, (blk, HEAD), 1)
    hi, mid, lo = _split3(k_pos)
    aug_k = jnp.where(k_lane == 0, hi, jnp.where(k_lane == 1, mid, jnp.where(
        k_lane == 2, lo, jnp.where(k_lane < 6, 1.0, 0.0)))).astype(BF16)
    q_pos = lax.broadcasted_iota(jnp.int32, (HEAD, blk), 1).astype(F32) * (-unit)
    q_row = lax.broadcasted_iota(jnp.int32, (HEAD, blk), 0)
    hi, mid, lo = _split3(q_pos)
    aug_q = jnp.where(q_row < 3, 1.0, jnp.where(q_row == 3, hi, jnp.where(
        q_row == 4, mid, jnp.where(q_row == 5, lo, 0.0)))).astype(BF16)

    kmean_ref[...] = jnp.zeros_like(kmean_ref)
    for n in range(nb):
        rows = slice(n * blk, (n + 1) * blk)
        kblk = k_ref[rows, :]
        kmean_ref[n:n + 1, :] = jnp.mean(kblk, axis=0, keepdims=True)
        kb_ref[rows, :HEAD] = kblk.astype(BF16)
        kb_ref[rows, HEAD:] = aug_k
        vt_ref[:, rows] = v_ref[rows, :].T.astype(BF16)

    causal = (lax.broadcasted_iota(jnp.int32, (blk, blk), 0)
              <= lax.broadcasted_iota(jnp.int32, (blk, blk), 1))

    def score(qb):
        qsl = slice(qb * blk, (qb + 1) * blk)
        qt = q_ref[qsl, :].T
        if qb <= topk:
            picked = [None] * qb
        else:
            rows_p = -(-qb // 8) * 8
            blk_row = lax.broadcasted_iota(jnp.int32, (rows_p, blk), 0)
            gate = _dot(kmean_ref[0:rows_p, :], qt, precision=lax.Precision.HIGHEST)
            gate = jnp.where(blk_row < qb, gate, -jnp.inf)
            sel = jnp.zeros((rows_p, blk), F32)
            for _ in range(topk):
                best = jnp.max(gate, axis=0, keepdims=True)
                first = jnp.min(jnp.where(gate == best, blk_row, rows_p), axis=0, keepdims=True)
                pick = blk_row == first
                sel = jnp.where(pick, 1.0, sel)
                gate = jnp.where(pick, -jnp.inf, gate)
            picked = [sel[n:n + 1, :] > 0.0 for n in range(qb)]

        nk = (qb + 1) * blk
        qt_aug = jnp.concatenate([qt.astype(BF16), aug_q], axis=0)
        raw = _dot(kb_ref[0:nk, :], qt_aug)
        return raw, picked

    def tile_max(qb, raw, picked):
        nk = (qb + 1) * blk
        tiles, shifts = [], []
        for n in range(qb):
            tiles.append(raw[n * blk:(n + 1) * blk, :] * to_log2)
            offset = slope2 * float((n - qb) * blk)
            shifts.append(offset if picked[n] is None else jnp.where(picked[n], offset, neg_inf))
        tiles.append(jnp.where(causal, raw[qb * blk:nk, :] * to_log2, -jnp.inf))
        shifts.append(0.0)
        m = None
        for t, sh in zip(tiles, shifts):
            top = jnp.max(t, axis=0, keepdims=True) + sh
            m = top if m is None else jnp.maximum(m, top)
        return tiles, [sh - m for sh in shifts]

    def softmax(qb, tiles, shifts):
        probs, denom = [], None
        for t, sh in zip(tiles, shifts):
            p = jnp.exp2(t + sh)
            probs.append(p.astype(BF16))
            part = jnp.sum(p, axis=0, keepdims=True)
            denom = part if denom is None else denom + part
        return jnp.concatenate(probs, axis=0), denom

    def output(qb, probs, denom):
        nk = (qb + 1) * blk
        acc = _dot(vt_ref[:, 0:nk], probs)
        o_ref[qb * blk:nk, :] = (acc / denom).T.astype(o_ref.dtype)

    stages = [score, tile_max, softmax, output]
    live = {}
    for step in range(nb + len(stages) - 1):
        for k, stage in enumerate(stages):
            qb = step - k
            if 0 <= qb < nb:
                args = live.pop((qb, k - 1)) if k else ()
                live[qb, k] = stage(qb, *args)


def _moba(proj):
    b, s, _ = proj.shape
    assert s % MOBA_BLOCK == 0
    nb = s // MOBA_BLOCK
    nbp = -(-nb // 8) * 8
    slopes = jnp.exp2(-8.0 * jnp.arange(1, N_HEADS + 1, dtype=F32) / N_HEADS)
    col = lambda off: pl.BlockSpec((None, s, HEAD), lambda bi, hi: (bi, 0, off + hi))
    return pl.pallas_call(
        _moba_kernel,
        grid=(b, N_HEADS),
        in_specs=[
            pl.BlockSpec(memory_space=pltpu.SMEM),
            pl.BlockSpec(memory_space=pltpu.SMEM),
            col(4 * N_HEADS), col(5 * N_HEADS), col(6 * N_HEADS),
        ],
        out_specs=pl.BlockSpec((None, s, HEAD), lambda bi, hi: (bi, 0, hi)),
        out_shape=jax.ShapeDtypeStruct((b, s, MIX_WIDTH), BF16),
        scratch_shapes=[
            pltpu.VMEM((s, 2 * HEAD), BF16),
            pltpu.VMEM((HEAD, s), BF16),
            pltpu.VMEM((nbp, HEAD), F32),
        ],
        compiler_params=pltpu.CompilerParams(
            dimension_semantics=("parallel", "parallel"),
            vmem_limit_bytes=VMEM_LIMIT),
        name="moba",
    )(slopes, slopes * HEAD ** 0.5, proj, proj, proj)


def kernel(x, ffn1_norm, ffn1_w_gate, ffn1_w_up, ffn1_w_down, mix_norm, w_in, hgrn_lower_bounds, hgrn_out_norm, w_out, ffn2_norm, ffn2_w_gate, ffn2_w_up, ffn2_w_down, final_norm):
    b, s, d = x.shape
    depth = ffn1_norm.shape[0]
    lb_all = jnp.cumsum(jax.nn.softmax(hgrn_lower_bounds.astype(F32), axis=0), axis=0)
    xs = x.reshape(b * s, d)
    for layer in range(depth):
        xs = _ffn(xs, ffn1_norm[layer], _prep_gate_up(ffn1_w_gate[layer], ffn1_w_up[layer]),
                  ffn1_w_down[layer].astype(BF16))
        proj = _in_proj(xs, mix_norm[layer], w_in[layer].astype(BF16), lb_all[layer])
        proj = proj.reshape(b, s, -1)
        o_hgrn = _hgrn(proj, hgrn_out_norm[layer]).reshape(b * s, MIX_WIDTH)
        o_moba = _moba(proj).reshape(b * s, MIX_WIDTH)
        xs = _out_proj(o_hgrn, o_moba, w_out[layer].astype(BF16), xs)
        last = layer == depth - 1
        xs = _ffn(xs, ffn2_norm[layer], _prep_gate_up(ffn2_w_gate[layer], ffn2_w_up[layer]),
                  ffn2_w_down[layer].astype(BF16), final_norm if last else None)
    if depth == 0:
        raise NotImplementedError("depth 0 has no FFN to carry the final norm")
    return xs.reshape(b, s, d)
```

```python
import functools

import jax
import jax.numpy as jnp
from jax import lax
from jax.experimental import pallas as pl
from jax.experimental.pallas import tpu as pltpu

F32 = jnp.float32
BF16 = jnp.bfloat16

HEAD = 128
N_HEADS = 16
MIX_WIDTH = N_HEADS * HEAD
MOBA_BLOCK = 256
MOBA_TOPK = 3
RMS_EPS = 1e-6
MACARON_WEIGHT = 0.5
LOG2E = 1.4426950408889634

HGRN_CHUNK = 64
SUBLANES = 8
HGRN_SUB = 8
HGRN_GROUP = 8
HGRN_SEQ_BLOCK = 1024
HGRN_UNROLL = 4
FF_TILE = 256
FF_PAIR = 2
NORM_ROWS = 16
NORM_UNROLL = 4

V7X_VMEM_BYTES = 64 * 1024 * 1024
VMEM_LIMIT = V7X_VMEM_BYTES - 6 * 1024 * 1024
FFN_VMEM_LIMIT = V7X_VMEM_BYTES - 1 * 1024 * 1024


def _dot(a, b, precision=None):
    return jnp.dot(a, b, preferred_element_type=F32, precision=precision)


def _dot_nt(a, b, precision=None):
    return lax.dot_general(a, b, (((1,), (1,)), ((), ())),
                           preferred_element_type=F32, precision=precision)


def _dot_tn(a, b):
    return lax.dot_general(a, b, (((0,), (0,)), ((), ())), preferred_element_type=F32)


def _rmsnorm_rows(x_ref, g_ref, o_ref):
    groups = x_ref.shape[0] // NORM_ROWS
    gain = g_ref[...]

    def rows(i):
        return pl.ds(pl.multiple_of(i * NORM_ROWS, NORM_ROWS), NORM_ROWS)

    def inv_rms(i):
        x = x_ref[rows(i), :]
        return lax.rsqrt(jnp.mean(x * x, axis=-1, keepdims=True) + RMS_EPS)

    def body(i, r):
        r_next = inv_rms(jnp.minimum(i + 1, groups - 1))
        o_ref[rows(i), :] = ((x_ref[rows(i), :] * r) * gain).astype(o_ref.dtype)
        return r_next

    lax.fori_loop(0, groups, body, inv_rms(0), unroll=NORM_UNROLL)


def _ffn_kernel(*refs, final_norm, n_tiles):
    if final_norm:
        x_ref, g_ref, wgu_ref, wd_ref, gf_ref, o_ref, h_ref = refs
    else:
        x_ref, g_ref, wgu_ref, wd_ref, o_ref, h_ref = refs
    f = pl.program_id(1)
    last = pl.num_programs(1) - 1
    tail = n_tiles - last * FF_PAIR

    def tile(k):
        gu = _dot(h_ref[...], wgu_ref[k])
        gate = gu[:, :FF_TILE]
        up = gu[:, FF_TILE:]
        act = (gate * jax.nn.sigmoid(gate)) * up * MACARON_WEIGHT
        return _dot(act.astype(BF16), wd_ref[k * FF_TILE:(k + 1) * FF_TILE, :])

    @pl.when(f == 0)
    def _():
        _rmsnorm_rows(x_ref, g_ref, h_ref)
        o_ref[...] = x_ref[...] + tile(0)
        for k in range(1, FF_PAIR):
            o_ref[...] += tile(k)

    @pl.when(jnp.logical_and(f > 0, f < last))
    def _():
        for k in range(FF_PAIR):
            o_ref[...] += tile(k)

    @pl.when(f == last)
    def _():
        for k in range(tail):
            o_ref[...] += tile(k)
        if final_norm:
            _rmsnorm_rows(o_ref, gf_ref, o_ref)


def _ffn(x, gain, wgu, wd, final_gain=None, *, tm=512):
    t, d = x.shape
    nf = wgu.shape[0]
    steps = pl.cdiv(nf, FF_PAIR)
    assert steps >= 2
    final_norm = final_gain is not None
    in_specs = [
        pl.BlockSpec((tm, d), lambda i, f: (i, 0)),
        pl.BlockSpec((1, d), lambda i, f: (0, 0)),
        pl.BlockSpec((FF_PAIR, d, 2 * FF_TILE), lambda i, f: (f, 0, 0)),
        pl.BlockSpec((FF_PAIR * FF_TILE, d), lambda i, f: (f, 0)),
    ]
    args = [x, gain.reshape(1, d), wgu, wd]
    if final_norm:
        in_specs.append(pl.BlockSpec((1, d), lambda i, f: (0, 0)))
        args.append(final_gain.reshape(1, d))
    return pl.pallas_call(
        functools.partial(_ffn_kernel, final_norm=final_norm, n_tiles=nf),
        grid=(t // tm, steps),
        in_specs=in_specs,
        out_specs=pl.BlockSpec((tm, d), lambda i, f: (i, 0)),
        out_shape=jax.ShapeDtypeStruct((t, d), F32),
        scratch_shapes=[pltpu.VMEM((tm, d), BF16)],
        compiler_params=pltpu.CompilerParams(
            dimension_semantics=("parallel", "arbitrary"),
            vmem_limit_bytes=FFN_VMEM_LIMIT),
        name="ffn_final" if final_norm else "ffn",
    )(*args)


def _pack_gate_up_kernel(wg_ref, wu_ref, o_ref):
    o_ref[:, :FF_TILE] = wg_ref[...].astype(BF16)
    o_ref[:, FF_TILE:] = wu_ref[...].astype(BF16)


def _prep_gate_up(w_gate, w_up):
    d, d_ff = w_gate.shape
    nf = d_ff // FF_TILE
    col = pl.BlockSpec((d, FF_TILE), lambda f: (0, f))
    return pl.pallas_call(
        _pack_gate_up_kernel,
        grid=(nf,),
        in_specs=[col, col],
        out_specs=pl.BlockSpec((None, d, 2 * FF_TILE), lambda f: (f, 0, 0)),
        out_shape=jax.ShapeDtypeStruct((nf, d, 2 * FF_TILE), BF16),
        compiler_params=pltpu.CompilerParams(
            dimension_semantics=("parallel",), vmem_limit_bytes=VMEM_LIMIT),
        name="pack_gate_up",
    )(w_gate, w_up)


def _in_proj_kernel(x_hbm, g_ref, w_ref, lb_ref, o_ref, x_ref, h_ref, sem, *, tiles_per_part):
    i = pl.program_id(0)
    j = pl.program_id(1)
    part = j // tiles_per_part
    tm = x_ref.shape[0]

    def fetch(tile):
        return pltpu.make_async_copy(x_hbm.at[pl.ds(tile * tm, tm), :], x_ref, sem)

    @pl.when(j == 0)
    def _():
        @pl.when(i == 0)
        def _():
            fetch(0).start()

        fetch(i).wait()
        _rmsnorm_rows(x_ref, g_ref, h_ref)

    @pl.when(jnp.logical_and(j == 1, i + 1 < pl.num_programs(0)))
    def _():
        fetch(i + 1).start()

    @pl.when(jnp.logical_or(part == 0, part == 3))
    def _():
        z = _dot(h_ref[...], w_ref[...])
        o_ref[...] = (z * jax.nn.sigmoid(z)) * jnp.where(part == 0, HEAD ** -0.5, 1.0)

    @pl.when(part == 1)
    def _():
        lb = lb_ref[...]
        o_ref[...] = lb + (1.0 - lb) * jax.nn.sigmoid(_dot(h_ref[...], w_ref[...]))

    @pl.when(jnp.logical_or(part == 2, part > 3))
    def _():
        o_ref[...] = _dot(h_ref[...], w_ref[...])


def _in_proj(x, gain, w, lower_bound, *, tm=1024, tn=1024):
    t, d = x.shape
    n = w.shape[1]
    tiles_per_part = MIX_WIDTH // tn
    assert n // tn >= 2 and t % tm == 0
    return pl.pallas_call(
        functools.partial(_in_proj_kernel, tiles_per_part=tiles_per_part),
        grid=(t // tm, n // tn),
        in_specs=[
            pl.BlockSpec(memory_space=pl.ANY),
            pl.BlockSpec((1, d), lambda i, j: (0, 0)),
            pl.BlockSpec((d, tn), lambda i, j: (0, j)),
            pl.BlockSpec((1, tn),
                         lambda i, j: (0, jnp.clip(j - tiles_per_part, 0, tiles_per_part - 1))),
        ],
        out_specs=pl.BlockSpec((tm, tn), lambda i, j: (i, j)),
        out_shape=jax.ShapeDtypeStruct((t, n), F32),
        scratch_shapes=[
            pltpu.VMEM((tm, d), F32),
            pltpu.VMEM((tm, d), BF16),
            pltpu.SemaphoreType.DMA(()),
        ],
        compiler_params=pltpu.CompilerParams(
            dimension_semantics=("arbitrary", "arbitrary"),
            vmem_limit_bytes=VMEM_LIMIT),
        name="in_proj",
    )(x, gain.reshape(1, d), w, lower_bound.reshape(1, MIX_WIDTH))


def _out_proj_kernel(oh_ref, om_ref, wh_ref, wm_ref, x_ref, o_ref):
    o_ref[...] = x_ref[...] + _dot(oh_ref[...], wh_ref[...]) + _dot(om_ref[...], wm_ref[...])


def _out_proj(o_hgrn, o_moba, w, x, *, tm=1024, tn=1024):
    t, d = x.shape
    kh = o_hgrn.shape[1]
    return pl.pallas_call(
        _out_proj_kernel,
        grid=(t // tm, d // tn),
        in_specs=[
            pl.BlockSpec((tm, kh), lambda i, j: (i, 0)),
            pl.BlockSpec((tm, kh), lambda i, j: (i, 0)),
            pl.BlockSpec((kh, tn), lambda i, j: (0, j)),
            pl.BlockSpec((kh, tn), lambda i, j: (1, j)),
            pl.BlockSpec((tm, tn), lambda i, j: (i, j)),
        ],
        out_specs=pl.BlockSpec((tm, tn), lambda i, j: (i, j)),
        out_shape=jax.ShapeDtypeStruct((t, d), F32),
        compiler_params=pltpu.CompilerParams(
            dimension_semantics=("parallel", "arbitrary"),
            vmem_limit_bytes=VMEM_LIMIT),
        name="out_proj",
    )(o_hgrn, o_moba, w, w, x)


def _hgrn_kernel(q_ref, f_ref, v_ref, gate_ref, gain_ref, o_ref,
                 st_ref, cum_ref, key_ref):
    c, sub, half = HGRN_CHUNK, HGRN_SUB, SUBLANES
    rows_total = q_ref.shape[0]
    gain = gain_ref[...]
    tri = (lax.broadcasted_iota(jnp.int32, (c, c), 0)
           >= lax.broadcasted_iota(jnp.int32, (c, c), 1)).astype(BF16)
    lane = lax.broadcasted_iota(jnp.int32, (half, c), 1)
    trow = lax.broadcasted_iota(jnp.int32, (half, HEAD), 0)

    @pl.when(pl.program_id(2) == 0)
    def _():
        st_ref[...] = jnp.zeros_like(st_ref)

    def chunk(ci, carry):
        sl = pl.ds(pl.multiple_of(ci * c, c), c)
        forget_all = f_ref[sl, :]
        logf = jnp.log2(forget_all)
        hi = logf.astype(BF16)
        rest = logf - hi.astype(F32)
        mid = rest.astype(BF16)
        lo = (rest - mid.astype(F32)).astype(BF16)
        cum_all = _dot(tri, hi) + _dot(tri, mid) + _dot(tri, lo)
        key_all = cum_all - jnp.log2(1.0 - forget_all)

        heads = range(HGRN_GROUP)
        cols = [slice(g * HEAD, (g + 1) * HEAD) for g in heads]
        cum = [cum_all[:, cols[g]] for g in heads]
        key = [key_all[:, cols[g]] for g in heads]
        q, vb, o = [], [], []
        for g in heads:
            q.append(q_ref[sl, cols[g]])
            vb.append(v_ref[sl, cols[g]].astype(BF16))
            cum_ref[g] = cum[g]
            key_ref[g] = key[g]
        for g in heads:
            last = cum[g][c - 1:c, :]
            st = st_ref[g]
            o.append(_dot_nt((q[g] * jnp.exp2(cum[g])).astype(BF16), st.astype(BF16)))
            k_end = jnp.exp2(last - key[g]).astype(BF16)
            st_ref[g] = st * jnp.exp2(last) + _dot_tn(vb[g], k_end)

        off = {}
        for i in range(1, c // sub):
            lo_row = i * sub
            for g in heads:
                edge = cum_ref[g, lo_row - 1:lo_row, :]
                q_t = (q[g][lo_row:lo_row + sub, :]
                       * jnp.exp2(cum[g][lo_row:lo_row + sub, :] - edge)).astype(BF16)
                k_t = jnp.exp2(edge - key[g][:lo_row, :])
                k_t = jnp.concatenate([k_t, jnp.zeros((c - lo_row, HEAD), F32)], axis=0)
                off[g, i] = _dot_nt(q_t, k_t.astype(BF16))

        blocks = [[] for _ in heads]
        for i in range(c // sub):
            lo_row = i * sub
            for h in range(sub // half):
                top = lo_row + h * half
                for g in heads:
                    cum_b = cum[g][top:top + half, :]
                    q_b = q[g][top:top + half, :]
                    a = jnp.zeros((half, c), F32)
                    for s in range((h + 1) * half):
                        r = lo_row + s
                        diff = cum_b - key_ref[g, r:r + 1, :]
                        if s > h * half:
                            diff = jnp.where(trow >= s - h * half, diff, -jnp.inf)
                        p = q_b * jnp.exp2(diff)
                        a = jnp.where(lane == r, jnp.sum(p, axis=-1, keepdims=True), a)
                    if i > 0:
                        a = a + off[g, i][h * half:(h + 1) * half, :]
                    blocks[g].append(a)

        for g in heads:
            scores = jnp.concatenate(blocks[g], axis=0)
            og = o[g] + _dot(scores.astype(BF16), vb[g])
            r = lax.rsqrt(jnp.mean(og * og, axis=-1, keepdims=True) + RMS_EPS)
            o_ref[sl, cols[g]] = (((og * r) * gain) * gate_ref[sl, cols[g]]).astype(o_ref.dtype)
        return carry

    lax.fori_loop(0, rows_total // c, chunk, 0, unroll=HGRN_UNROLL)


def _hgrn(proj, norm_gain):
    b, s, _ = proj.shape
    grp = HGRN_GROUP
    ng = N_HEADS // grp
    sb = min(HGRN_SEQ_BLOCK, s)
    assert s % sb == 0 and sb % (HGRN_CHUNK * HGRN_UNROLL) == 0
    col = lambda off: pl.BlockSpec((None, sb, grp * HEAD), lambda bi, gi, si: (bi, si, off + gi))
    return pl.pallas_call(
        _hgrn_kernel,
        grid=(b, ng, s // sb),
        in_specs=[
            col(0), col(ng), col(2 * ng), col(3 * ng),
            pl.BlockSpec((1, HEAD), lambda bi, gi, si: (0, 0)),
        ],
        out_specs=pl.BlockSpec((None, sb, grp * HEAD), lambda bi, gi, si: (bi, si, gi)),
        out_shape=jax.ShapeDtypeStruct((b, s, MIX_WIDTH), BF16),
        scratch_shapes=[
            pltpu.VMEM((grp, HEAD, HEAD), F32),
            pltpu.VMEM((grp, HGRN_CHUNK, HEAD), F32),
            pltpu.VMEM((grp, HGRN_CHUNK, HEAD), F32),
        ],
        compiler_params=pltpu.CompilerParams(
            dimension_semantics=("parallel", "parallel", "arbitrary"),
            vmem_limit_bytes=VMEM_LIMIT),
        name="hgrn2",
    )(proj, proj, proj, proj, norm_gain.reshape(1, HEAD))


def _split3(v):
    hi = v.astype(BF16).astype(F32)
    mid = (v - hi).astype(BF16).astype(F32)
    lo = ((v - hi) - mid).astype(BF16).astype(F32)
    return hi, mid, lo


def _moba_kernel(slopes_ref, units_ref, q_ref, k_ref, v_ref, o_ref, kb_ref, vt_ref, kmean_ref):
    blk = MOBA_BLOCK
    seq = q_ref.shape[0]
    nb = seq // blk
    topk = min(MOBA_TOPK, max(nb - 1, 1))
    head = pl.program_id(1)
    to_log2 = (HEAD ** -0.5) * LOG2E
    slope2 = slopes_ref[head] * LOG2E
    unit = units_ref[head]
    neg_inf = jnp.full((1, blk), -jnp.inf, F32)

    k_pos = lax.broadcasted_iota(jnp.int32, (blk, HEAD), 0).astype(F32) * unit
    k_lane = lax.broadcasted_iota(jnp.int32, (blk, HEAD), 1)
    hi, mid, lo = _split3(k_pos)
    aug_k = jnp.where(k_lane == 0, hi, jnp.where(k_lane == 1, mid, jnp.where(
        k_lane == 2, lo, jnp.where(k_lane < 6, 1.0, 0.0)))).astype(BF16)
    q_pos = lax.broadcasted_iota(jnp.int32, (HEAD, blk), 1).astype(F32) * (-unit)
    q_row = lax.broadcasted_iota(jnp.int32, (HEAD, blk), 0)
    hi, mid, lo = _split3(q_pos)
    aug_q = jnp.where(q_row < 3, 1.0, jnp.where(q_row == 3, hi, jnp.where(
        q_row == 4, mid, jnp.where(q_row == 5, lo, 0.0)))).astype(BF16)

    kmean_ref[...] = jnp.zeros_like(kmean_ref)
    for n in range(nb):
        rows = slice(n * blk, (n + 1) * blk)
        kblk = k_ref[rows, :]
        kmean_ref[n:n + 1, :] = jnp.mean(kblk, axis=0, keepdims=True)
        kb_ref[rows, :HEAD] = kblk.astype(BF16)
        kb_ref[rows, HEAD:] = aug_k
        vt_ref[:, rows] = v_ref[rows, :].T.astype(BF16)

    causal = (lax.broadcasted_iota(jnp.int32, (blk, blk), 0)
              <= lax.broadcasted_iota(jnp.int32, (blk, blk), 1))

    def score(qb):
        qsl = slice(qb * blk, (qb + 1) * blk)
        qt = q_ref[qsl, :].T
        if qb <= topk:
            picked = [None] * qb
        else:
            rows_p = -(-qb // 8) * 8
            blk_row = lax.broadcasted_iota(jnp.int32, (rows_p, blk), 0)
            gate = _dot(kmean_ref[0:rows_p, :], qt, precision=lax.Precision.HIGHEST)
            gate = jnp.where(blk_row < qb, gate, -jnp.inf)
            sel = jnp.zeros((rows_p, blk), F32)
            for _ in range(topk):
                best = jnp.max(gate, axis=0, keepdims=True)
                first = jnp.min(jnp.where(gate == best, blk_row, rows_p), axis=0, keepdims=True)
                pick = blk_row == first
                sel = jnp.where(pick, 1.0, sel)
                gate = jnp.where(pick, -jnp.inf, gate)
            picked = [sel[n:n + 1, :] > 0.0 for n in range(qb)]

        nk = (qb + 1) * blk
        qt_aug = jnp.concatenate([qt.astype(BF16), aug_q], axis=0)
        raw = _dot(kb_ref[0:nk, :], qt_aug)
        return raw, picked

    def tile_max(qb, raw, picked):
        nk = (qb + 1) * blk
        tiles, shifts = [], []
        for n in range(qb):
            tiles.append(raw[n * blk:(n + 1) * blk, :] * to_log2)
            offset = slope2 * float((n - qb) * blk)
            shifts.append(offset if picked[n] is None else jnp.where(picked[n], offset, neg_inf))
        tiles.append(jnp.where(causal, raw[qb * blk:nk, :] * to_log2, -jnp.inf))
        shifts.append(0.0)
        m = None
        for t, sh in zip(tiles, shifts):
            top = jnp.max(t, axis=0, keepdims=True) + sh
            m = top if m is None else jnp.maximum(m, top)
        return tiles, [sh - m for sh in shifts]

    def softmax(qb, tiles, shifts):
        probs, denom = [], None
        for t, sh in zip(tiles, shifts):
            p = jnp.exp2(t + sh)
            probs.append(p.astype(BF16))
            part = jnp.sum(p, axis=0, keepdims=True)
            denom = part if denom is None else denom + part
        return jnp.concatenate(probs, axis=0), denom

    def output(qb, probs, denom):
        nk = (qb + 1) * blk
        acc = _dot(vt_ref[:, 0:nk], probs)
        o_ref[qb * blk:nk, :] = (acc / denom).T.astype(o_ref.dtype)

    stages = [score, tile_max, softmax, output]
    live = {}
    for step in range(nb + len(stages) - 1):
        for k, stage in enumerate(stages):
            qb = step - k
            if 0 <= qb < nb:
                args = live.pop((qb, k - 1)) if k else ()
                live[qb, k] = stage(qb, *args)


def _moba(proj):
    b, s, _ = proj.shape
    assert s % MOBA_BLOCK == 0
    nb = s // MOBA_BLOCK
    nbp = -(-nb // 8) * 8
    slopes = jnp.exp2(-8.0 * jnp.arange(1, N_HEADS + 1, dtype=F32) / N_HEADS)
    col = lambda off: pl.BlockSpec((None, s, HEAD), lambda bi, hi: (bi, 0, off + hi))
    return pl.pallas_call(
        _moba_kernel,
        grid=(b, N_HEADS),
        in_specs=[
            pl.BlockSpec(memory_space=pltpu.SMEM),
            pl.BlockSpec(memory_space=pltpu.SMEM),
            col(4 * N_HEADS), col(5 * N_HEADS), col(6 * N_HEADS),
        ],
        out_specs=pl.BlockSpec((None, s, HEAD), lambda bi, hi: (bi, 0, hi)),
        out_shape=jax.ShapeDtypeStruct((b, s, MIX_WIDTH), BF16),
        scratch_shapes=[
            pltpu.VMEM((s, 2 * HEAD), BF16),
            pltpu.VMEM((HEAD, s), BF16),
            pltpu.VMEM((nbp, HEAD), F32),
        ],
        compiler_params=pltpu.CompilerParams(
            dimension_semantics=("parallel", "parallel"),
            vmem_limit_bytes=VMEM_LIMIT),
        name="moba",
    )(slopes, slopes * HEAD ** 0.5, proj, proj, proj)


def kernel(x, ffn1_norm, ffn1_w_gate, ffn1_w_up, ffn1_w_down, mix_norm, w_in, hgrn_lower_bounds, hgrn_out_norm, w_out, ffn2_norm, ffn2_w_gate, ffn2_w_up, ffn2_w_down, final_norm):
    b, s, d = x.shape
    depth = ffn1_norm.shape[0]
    lb_all = jnp.cumsum(jax.nn.softmax(hgrn_lower_bounds.astype(F32), axis=0), axis=0)
    xs = x.reshape(b * s, d)
    for layer in range(depth):
        xs = _ffn(xs, ffn1_norm[layer], _prep_gate_up(ffn1_w_gate[layer], ffn1_w_up[layer]),
                  ffn1_w_down[layer].astype(BF16))
        proj = _in_proj(xs, mix_norm[layer], w_in[layer].astype(BF16), lb_all[layer])
        proj = proj.reshape(b, s, -1)
        o_hgrn = _hgrn(proj, hgrn_out_norm[layer]).reshape(b * s, MIX_WIDTH)
        o_moba = _moba(proj).reshape(b * s, MIX_WIDTH)
        xs = _out_proj(o_hgrn, o_moba, w_out[layer].astype(BF16), xs)
        last = layer == depth - 1
        xs = _ffn(xs, ffn2_norm[layer], _prep_gate_up(ffn2_w_gate[layer], ffn2_w_up[layer]),
                  ffn2_w_down[layer].astype(BF16), final_norm if last else None)
    if depth == 0:
        raise NotImplementedError("depth 0 has no FFN to carry the final norm")
    return xs.reshape(b, s, d)
```

```python
import functools

import jax
import jax.numpy as jnp
from jax import lax
from jax.experimental import pallas as pl
from jax.experimental.pallas import tpu as pltpu

F32 = jnp.float32
BF16 = jnp.bfloat16

HEAD = 128
N_HEADS = 16
MIX_WIDTH = N_HEADS * HEAD
MOBA_BLOCK = 256
MOBA_TOPK = 3
RMS_EPS = 1e-6
MACARON_WEIGHT = 0.5
LOG2E = 1.4426950408889634

HGRN_CHUNK = 64
SUBLANES = 8
HGRN_SUB = 8
HGRN_GROUP = 8
HGRN_SEQ_BLOCK = 1024
HGRN_UNROLL = 4
FF_TILE = 256
FF_PAIR = 2
NORM_ROWS = 16
NORM_UNROLL = 4

V7X_VMEM_BYTES = 64 * 1024 * 1024
VMEM_LIMIT = V7X_VMEM_BYTES - 6 * 1024 * 1024
FFN_VMEM_LIMIT = V7X_VMEM_BYTES - 1 * 1024 * 1024


def _dot(a, b, precision=None):
    return jnp.dot(a, b, preferred_element_type=F32, precision=precision)


def _dot_nt(a, b, precision=None):
    return lax.dot_general(a, b, (((1,), (1,)), ((), ())),
                           preferred_element_type=F32, precision=precision)


def _dot_tn(a, b):
    return lax.dot_general(a, b, (((0,), (0,)), ((), ())), preferred_element_type=F32)


def _rmsnorm_rows(x_ref, g_ref, o_ref, unroll=NORM_UNROLL):
    groups = x_ref.shape[0] // NORM_ROWS
    gain = g_ref[...]

    def rows(i):
        return pl.ds(pl.multiple_of(i * NORM_ROWS, NORM_ROWS), NORM_ROWS)

    def inv_rms(i):
        x = x_ref[rows(i), :]
        return lax.rsqrt(jnp.mean(x * x, axis=-1, keepdims=True) + RMS_EPS)

    def body(i, r):
        r_next = inv_rms(jnp.minimum(i + 1, groups - 1))
        o_ref[rows(i), :] = ((x_ref[rows(i), :] * r) * gain).astype(o_ref.dtype)
        return r_next

    lax.fori_loop(0, groups, body, inv_rms(0), unroll=unroll)


def _ffn_kernel(*refs, final_norm, n_tiles):
    if final_norm:
        x_hbm, g_ref, wgu_ref, wd_ref, gf_ref, o_ref, x_ref, h_ref, sem = refs
    else:
        x_hbm, g_ref, wgu_ref, wd_ref, o_ref, x_ref, h_ref, sem = refs
    i = pl.program_id(0)
    f = pl.program_id(1)
    last = pl.num_programs(1) - 1
    more_rows = i + 1 < pl.num_programs(0)
    tail = n_tiles - last * FF_PAIR
    slot = i % 2
    tm = x_ref.shape[0]

    def fetch(row_tile):
        return pltpu.make_async_copy(x_hbm.at[pl.ds(row_tile * tm, tm), :], x_ref, sem)

    def tile(k):
        gu = _dot(h_ref[slot], wgu_ref[k])
        gate = gu[:, :FF_TILE]
        up = gu[:, FF_TILE:]
        act = (gate * jax.nn.sigmoid(gate)) * up * MACARON_WEIGHT
        return _dot(act.astype(BF16), wd_ref[k * FF_TILE:(k + 1) * FF_TILE, :])

    @pl.when(f == 0)
    def _():
        @pl.when(i == 0)
        def _():
            fetch(0).start()
            fetch(0).wait()
            _rmsnorm_rows(x_ref, g_ref, h_ref.at[0])

        o_ref[...] = x_ref[...] + tile(0)
        for k in range(1, FF_PAIR):
            o_ref[...] += tile(k)

    @pl.when(jnp.logical_and(f == 1, more_rows))
    def _():
        fetch(i + 1).start()

    @pl.when(jnp.logical_and(f > 0, f < last))
    def _():
        for k in range(FF_PAIR):
            o_ref[...] += tile(k)

    @pl.when(f == last)
    def _():
        @pl.when(more_rows)
        def _():
            fetch(i + 1).wait()

        _rmsnorm_rows(x_ref, g_ref, h_ref.at[1 - slot], unroll=True)
        for k in range(tail):
            o_ref[...] += tile(k)
        if final_norm:
            _rmsnorm_rows(o_ref, gf_ref, o_ref)


def _ffn(x, gain, wgu, wd, final_gain=None, *, tm=512):
    t, d = x.shape
    nf = wgu.shape[0]
    steps = pl.cdiv(nf, FF_PAIR)
    assert steps >= 2 and t % tm == 0
    final_norm = final_gain is not None
    in_specs = [
        pl.BlockSpec(memory_space=pl.ANY),
        pl.BlockSpec((1, d), lambda i, f: (0, 0)),
        pl.BlockSpec((FF_PAIR, d, 2 * FF_TILE), lambda i, f: (f, 0, 0)),
        pl.BlockSpec((FF_PAIR * FF_TILE, d), lambda i, f: (f, 0)),
    ]
    args = [x, gain.reshape(1, d), wgu, wd]
    if final_norm:
        in_specs.append(pl.BlockSpec((1, d), lambda i, f: (0, 0)))
        args.append(final_gain.reshape(1, d))
    return pl.pallas_call(
        functools.partial(_ffn_kernel, final_norm=final_norm, n_tiles=nf),
        grid=(t // tm, steps),
        in_specs=in_specs,
        out_specs=pl.BlockSpec((tm, d), lambda i, f: (i, 0)),
        out_shape=jax.ShapeDtypeStruct((t, d), F32),
        scratch_shapes=[
            pltpu.VMEM((tm, d), F32),
            pltpu.VMEM((2, tm, d), BF16),
            pltpu.SemaphoreType.DMA(()),
        ],
        compiler_params=pltpu.CompilerParams(
            dimension_semantics=("arbitrary", "arbitrary"),
            vmem_limit_bytes=FFN_VMEM_LIMIT),
        name="ffn_final" if final_norm else "ffn",
    )(*args)


def _pack_gate_up_kernel(wg_ref, wu_ref, o_ref):
    o_ref[:, :FF_TILE] = wg_ref[...].astype(BF16)
    o_ref[:, FF_TILE:] = wu_ref[...].astype(BF16)


def _prep_gate_up(w_gate, w_up):
    d, d_ff = w_gate.shape
    nf = d_ff // FF_TILE
    col = pl.BlockSpec((d, FF_TILE), lambda f: (0, f))
    return pl.pallas_call(
        _pack_gate_up_kernel,
        grid=(nf,),
        in_specs=[col, col],
        out_specs=pl.BlockSpec((None, d, 2 * FF_TILE), lambda f: (f, 0, 0)),
        out_shape=jax.ShapeDtypeStruct((nf, d, 2 * FF_TILE), BF16),
        compiler_params=pltpu.CompilerParams(
            dimension_semantics=("parallel",), vmem_limit_bytes=VMEM_LIMIT),
        name="pack_gate_up",
    )(w_gate, w_up)


def _in_proj_kernel(x_hbm, g_ref, w_ref, lb_ref, o_ref, x_ref, h_ref, sem, *, tiles_per_part):
    i = pl.program_id(0)
    j = pl.program_id(1)
    part = j // tiles_per_part
    tm = x_ref.shape[0]

    def fetch(tile):
        return pltpu.make_async_copy(x_hbm.at[pl.ds(tile * tm, tm), :], x_ref, sem)

    @pl.when(j == 0)
    def _():
        @pl.when(i == 0)
        def _():
            fetch(0).start()

        fetch(i).wait()
        _rmsnorm_rows(x_ref, g_ref, h_ref)

    @pl.when(jnp.logical_and(j == 1, i + 1 < pl.num_programs(0)))
    def _():
        fetch(i + 1).start()

    @pl.when(jnp.logical_or(part == 0, part == 3))
    def _():
        z = _dot(h_ref[...], w_ref[...])
        o_ref[...] = (z * jax.nn.sigmoid(z)) * jnp.where(part == 0, HEAD ** -0.5, 1.0)

    @pl.when(part == 1)
    def _():
        lb = lb_ref[...]
        o_ref[...] = lb + (1.0 - lb) * jax.nn.sigmoid(_dot(h_ref[...], w_ref[...]))

    @pl.when(jnp.logical_or(part == 2, part > 3))
    def _():
        o_ref[...] = _dot(h_ref[...], w_ref[...])


def _in_proj(x, gain, w, lower_bound, *, tm=1024, tn=1024):
    t, d = x.shape
    n = w.shape[1]
    tiles_per_part = MIX_WIDTH // tn
    assert n // tn >= 2 and t % tm == 0
    return pl.pallas_call(
        functools.partial(_in_proj_kernel, tiles_per_part=tiles_per_part),
        grid=(t // tm, n // tn),
        in_specs=[
            pl.BlockSpec(memory_space=pl.ANY),
            pl.BlockSpec((1, d), lambda i, j: (0, 0)),
            pl.BlockSpec((d, tn), lambda i, j: (0, j)),
            pl.BlockSpec((1, tn),
                         lambda i, j: (0, jnp.clip(j - tiles_per_part, 0, tiles_per_part - 1))),
        ],
        out_specs=pl.BlockSpec((tm, tn), lambda i, j: (i, j)),
        out_shape=jax.ShapeDtypeStruct((t, n), F32),
        scratch_shapes=[
            pltpu.VMEM((tm, d), F32),
            pltpu.VMEM((tm, d), BF16),
            pltpu.SemaphoreType.DMA(()),
        ],
        compiler_params=pltpu.CompilerParams(
            dimension_semantics=("arbitrary", "arbitrary"),
            vmem_limit_bytes=VMEM_LIMIT),
        name="in_proj",
    )(x, gain.reshape(1, d), w, lower_bound.reshape(1, MIX_WIDTH))


def _out_proj_kernel(oh_ref, om_ref, wh_ref, wm_ref, x_ref, o_ref):
    o_ref[...] = x_ref[...] + _dot(oh_ref[...], wh_ref[...]) + _dot(om_ref[...], wm_ref[...])


def _out_proj(o_hgrn, o_moba, w, x, *, tm=1024, tn=1024):
    t, d = x.shape
    kh = o_hgrn.shape[1]
    return pl.pallas_call(
        _out_proj_kernel,
        grid=(t // tm, d // tn),
        in_specs=[
            pl.BlockSpec((tm, kh), lambda i, j: (i, 0)),
            pl.BlockSpec((tm, kh), lambda i, j: (i, 0)),
            pl.BlockSpec((kh, tn), lambda i, j: (0, j)),
            pl.BlockSpec((kh, tn), lambda i, j: (1, j)),
            pl.BlockSpec((tm, tn), lambda i, j: (i, j)),
        ],
        out_specs=pl.BlockSpec((tm, tn), lambda i, j: (i, j)),
        out_shape=jax.ShapeDtypeStruct((t, d), F32),
        compiler_params=pltpu.CompilerParams(
            dimension_semantics=("parallel", "arbitrary"),
            vmem_limit_bytes=VMEM_LIMIT),
        name="out_proj",
    )(o_hgrn, o_moba, w, w, x)


def _hgrn_kernel(q_ref, f_ref, v_ref, gate_ref, gain_ref, o_ref,
                 st_ref, cum_ref, key_ref):
    c, sub, half = HGRN_CHUNK, HGRN_SUB, SUBLANES
    rows_total = q_ref.shape[0]
    gain = gain_ref[...]
    tri = (lax.broadcasted_iota(jnp.int32, (c, c), 0)
           >= lax.broadcasted_iota(jnp.int32, (c, c), 1)).astype(BF16)
    lane = lax.broadcasted_iota(jnp.int32, (half, c), 1)
    trow = lax.broadcasted_iota(jnp.int32, (half, HEAD), 0)

    @pl.when(pl.program_id(2) == 0)
    def _():
        st_ref[...] = jnp.zeros_like(st_ref)

    def chunk(ci, carry):
        sl = pl.ds(pl.multiple_of(ci * c, c), c)
        forget_all = f_ref[sl, :]
        logf = jnp.log2(forget_all)
        hi = logf.astype(BF16)
        rest = logf - hi.astype(F32)
        mid = rest.astype(BF16)
        lo = (rest - mid.astype(F32)).astype(BF16)
        cum_all = _dot(tri, hi) + _dot(tri, mid) + _dot(tri, lo)
        key_all = cum_all - jnp.log2(1.0 - forget_all)

        heads = range(HGRN_GROUP)
        cols = [slice(g * HEAD, (g + 1) * HEAD) for g in heads]
        cum = [cum_all[:, cols[g]] for g in heads]
        key = [key_all[:, cols[g]] for g in heads]
        q, vb, o = [], [], []
        for g in heads:
            q.append(q_ref[sl, cols[g]])
            vb.append(v_ref[sl, cols[g]].astype(BF16))
            cum_ref[g] = cum[g]
            key_ref[g] = key[g]
        for g in heads:
            last = cum[g][c - 1:c, :]
            st = st_ref[g]
            o.append(_dot_nt((q[g] * jnp.exp2(cum[g])).astype(BF16), st.astype(BF16)))
            k_end = jnp.exp2(last - key[g]).astype(BF16)
            st_ref[g] = st * jnp.exp2(last) + _dot_tn(vb[g], k_end)

        off = {}
        for i in range(1, c // sub):
            lo_row = i * sub
            for g in heads:
                edge = cum_ref[g, lo_row - 1:lo_row, :]
                q_t = (q[g][lo_row:lo_row + sub, :]
                       * jnp.exp2(cum[g][lo_row:lo_row + sub, :] - edge)).astype(BF16)
                k_t = jnp.exp2(edge - key[g][:lo_row, :])
                k_t = jnp.concatenate([k_t, jnp.zeros((c - lo_row, HEAD), F32)], axis=0)
                off[g, i] = _dot_nt(q_t, k_t.astype(BF16))

        blocks = [[] for _ in heads]
        for i in range(c // sub):
            lo_row = i * sub
            for h in range(sub // half):
                top = lo_row + h * half
                for g in heads:
                    cum_b = cum[g][top:top + half, :]
                    q_b = q[g][top:top + half, :]
                    a = jnp.zeros((half, c), F32)
                    for s in range((h + 1) * half):
                        r = lo_row + s
                        diff = cum_b - key_ref[g, r:r + 1, :]
                        if s > h * half:
                            diff = jnp.where(trow >= s - h * half, diff, -jnp.inf)
                        p = q_b * jnp.exp2(diff)
                        a = jnp.where(lane == r, jnp.sum(p, axis=-1, keepdims=True), a)
                    if i > 0:
                        a = a + off[g, i][h * half:(h + 1) * half, :]
                    blocks[g].append(a)

        for g in heads:
            scores = jnp.concatenate(blocks[g], axis=0)
            og = o[g] + _dot(scores.astype(BF16), vb[g])
            r = lax.rsqrt(jnp.mean(og * og, axis=-1, keepdims=True) + RMS_EPS)
            o_ref[sl, cols[g]] = (((og * r) * gain) * gate_ref[sl, cols[g]]).astype(o_ref.dtype)
        return carry

    lax.fori_loop(0, rows_total // c, chunk, 0, unroll=HGRN_UNROLL)


def _hgrn(proj, norm_gain):
    b, s, _ = proj.shape
    grp = HGRN_GROUP
    ng = N_HEADS // grp
    sb = min(HGRN_SEQ_BLOCK, s)
    assert s % sb == 0 and sb % (HGRN_CHUNK * HGRN_UNROLL) == 0
    col = lambda off: pl.BlockSpec((None, sb, grp * HEAD), lambda bi, gi, si: (bi, si, off + gi))
    return pl.pallas_call(
        _hgrn_kernel,
        grid=(b, ng, s // sb),
        in_specs=[
            col(0), col(ng), col(2 * ng), col(3 * ng),
            pl.BlockSpec((1, HEAD), lambda bi, gi, si: (0, 0)),
        ],
        out_specs=pl.BlockSpec((None, sb, grp * HEAD), lambda bi, gi, si: (bi, si, gi)),
        out_shape=jax.ShapeDtypeStruct((b, s, MIX_WIDTH), BF16),
        scratch_shapes=[
            pltpu.VMEM((grp, HEAD, HEAD), F32),
            pltpu.VMEM((grp, HGRN_CHUNK, HEAD), F32),
            pltpu.VMEM((grp, HGRN_CHUNK, HEAD), F32),
        ],
        compiler_params=pltpu.CompilerParams(
            dimension_semantics=("parallel", "parallel", "arbitrary"),
            vmem_limit_bytes=VMEM_LIMIT),
        name="hgrn2",
    )(proj, proj, proj, proj, norm_gain.reshape(1, HEAD))


def _split3(v):
    hi = v.astype(BF16).astype(F32)
    mid = (v - hi).astype(BF16).astype(F32)
    lo = ((v - hi) - mid).astype(BF16).astype(F32)
    return hi, mid, lo


def _moba_kernel(slopes_ref, units_ref, q_ref, k_ref, v_ref, o_ref, kb_ref, vt_ref, kmean_ref):
    blk = MOBA_BLOCK
    seq = q_ref.shape[0]
    nb = seq // blk
    topk = min(MOBA_TOPK, max(nb - 1, 1))
    head = pl.program_id(1)
    to_log2 = (HEAD ** -0.5) * LOG2E
    slope2 = slopes_ref[head] * LOG2E
    unit = units_ref[head]
    neg_inf = jnp.full((1, blk), -jnp.inf, F32)

    k_pos = lax.broadcasted_iota(jnp.int32, (blk, HEAD), 0).astype(F32) * unit
    k_lane = lax.broadcasted_iota(jnp.int32, (blk, HEAD), 1)
    hi, mid, lo = _split3(k_pos)
    aug_k = jnp.where(k_lane == 0, hi, jnp.where(k_lane == 1, mid, jnp.where(
        k_lane == 2, lo, jnp.where(k_lane < 6, 1.0, 0.0)))).astype(BF16)
    q_pos = lax.broadcasted_iota(jnp.int32, (HEAD, blk), 1).astype(F32) * (-unit)
    q_row = lax.broadcasted_iota(jnp.int32, (HEAD, blk), 0)
    hi, mid, lo = _split3(q_pos)
    aug_q = jnp.where(q_row < 3, 1.0, jnp.where(q_row == 3, hi, jnp.where(
        q_row == 4, mid, jnp.where(q_row == 5, lo, 0.0)))).astype(BF16)

    kmean_ref[...] = jnp.zeros_like(kmean_ref)
    for n in range(nb):
        rows = slice(n * blk, (n + 1) * blk)
        kblk = k_ref[rows, :]
        kmean_ref[n:n + 1, :] = jnp.mean(kblk, axis=0, keepdims=True)
        kb_ref[rows, :HEAD] = kblk.astype(BF16)
        kb_ref[rows, HEAD:] = aug_k
        vt_ref[:, rows] = v_ref[rows, :].T.astype(BF16)

    causal = (lax.broadcasted_iota(jnp.int32, (blk, blk), 0)
              <= lax.broadcasted_iota(jnp.int32, (blk, blk), 1))

    def score(qb):
        qsl = slice(qb * blk, (qb + 1) * blk)
        qt = q_ref[qsl, :].T
        if qb <= topk:
            picked = [None] * qb
        else:
            rows_p = -(-qb // 8) * 8
            blk_row = lax.broadcasted_iota(jnp.int32, (rows_p, blk), 0)
            gate = _dot(kmean_ref[0:rows_p, :], qt, precision=lax.Precision.HIGHEST)
            gate = jnp.where(blk_row < qb, gate, -jnp.inf)
            sel = jnp.zeros((rows_p, blk), F32)
            for _ in range(topk):
                best = jnp.max(gate, axis=0, keepdims=True)
                first = jnp.min(jnp.where(gate == best, blk_row, rows_p), axis=0, keepdims=True)
                pick = blk_row == first
                sel = jnp.where(pick, 1.0, sel)
                gate = jnp.where(pick, -jnp.inf, gate)
            picked = [sel[n:n + 1, :] > 0.0 for n in range(qb)]

        nk = (qb + 1) * blk
        qt_aug = jnp.concatenate([qt.astype(BF16), aug_q], axis=0)
        raw = _dot(kb_ref[0:nk, :], qt_aug)
        return raw, picked

    def tile_max(qb, raw, picked):
        nk = (qb + 1) * blk
        tiles, shifts = [], []
        for n in range(qb):
            tiles.append(raw[n * blk:(n + 1) * blk, :] * to_log2)
            offset = slope2 * float((n - qb) * blk)
            shifts.append(offset if picked[n] is None else jnp.where(picked[n], offset, neg_inf))
        tiles.append(jnp.where(causal, raw[qb * blk:nk, :] * to_log2, -jnp.inf))
        shifts.append(0.0)
        m = None
        for t, sh in zip(tiles, shifts):
            top = jnp.max(t, axis=0, keepdims=True) + sh
            m = top if m is None else jnp.maximum(m, top)
        return tiles, [sh - m for sh in shifts]

    def softmax(qb, tiles, shifts):
        probs, denom = [], None
        for t, sh in zip(tiles, shifts):
            p = jnp.exp2(t + sh)
            probs.append(p.astype(BF16))
            part = jnp.sum(p, axis=0, keepdims=True)
            denom = part if denom is None else denom + part
        return jnp.concatenate(probs, axis=0), denom

    def output(qb, probs, denom):
        nk = (qb + 1) * blk
        acc = _dot(vt_ref[:, 0:nk], probs)
        o_ref[qb * blk:nk, :] = (acc / denom).T.astype(o_ref.dtype)

    stages = [score, tile_max, softmax, output]
    live = {}
    for step in range(nb + len(stages) - 1):
        for k, stage in enumerate(stages):
            qb = step - k
            if 0 <= qb < nb:
                args = live.pop((qb, k - 1)) if k else ()
                live[qb, k] = stage(qb, *args)


def _moba(proj):
    b, s, _ = proj.shape
    assert s % MOBA_BLOCK == 0
    nb = s // MOBA_BLOCK
    nbp = -(-nb // 8) * 8
    slopes = jnp.exp2(-8.0 * jnp.arange(1, N_HEADS + 1, dtype=F32) / N_HEADS)
    col = lambda off: pl.BlockSpec((None, s, HEAD), lambda bi, hi: (bi, 0, off + hi))
    return pl.pallas_call(
        _moba_kernel,
        grid=(b, N_HEADS),
        in_specs=[
            pl.BlockSpec(memory_space=pltpu.SMEM),
            pl.BlockSpec(memory_space=pltpu.SMEM),
            col(4 * N_HEADS), col(5 * N_HEADS), col(6 * N_HEADS),
        ],
        out_specs=pl.BlockSpec((None, s, HEAD), lambda bi, hi: (bi, 0, hi)),
        out_shape=jax.ShapeDtypeStruct((b, s, MIX_WIDTH), BF16),
        scratch_shapes=[
            pltpu.VMEM((s, 2 * HEAD), BF16),
            pltpu.VMEM((HEAD, s), BF16),
            pltpu.VMEM((nbp, HEAD), F32),
        ],
        compiler_params=pltpu.CompilerParams(
            dimension_semantics=("parallel", "parallel"),
            vmem_limit_bytes=VMEM_LIMIT),
        name="moba",
    )(slopes, slopes * HEAD ** 0.5, proj, proj, proj)


def kernel(x, ffn1_norm, ffn1_w_gate, ffn1_w_up, ffn1_w_down, mix_norm, w_in, hgrn_lower_bounds, hgrn_out_norm, w_out, ffn2_norm, ffn2_w_gate, ffn2_w_up, ffn2_w_down, final_norm):
    b, s, d = x.shape
    depth = ffn1_norm.shape[0]
    lb_all = jnp.cumsum(jax.nn.softmax(hgrn_lower_bounds.astype(F32), axis=0), axis=0)
    xs = x.reshape(b * s, d)
    for layer in range(depth):
        xs = _ffn(xs, ffn1_norm[layer], _prep_gate_up(ffn1_w_gate[layer], ffn1_w_up[layer]),
                  ffn1_w_down[layer].astype(BF16))
        proj = _in_proj(xs, mix_norm[layer], w_in[layer].astype(BF16), lb_all[layer])
        proj = proj.reshape(b, s, -1)
        o_hgrn = _hgrn(proj, hgrn_out_norm[layer]).reshape(b * s, MIX_WIDTH)
        o_moba = _moba(proj).reshape(b * s, MIX_WIDTH)
        xs = _out_proj(o_hgrn, o_moba, w_out[layer].astype(BF16), xs)
        last = layer == depth - 1
        xs = _ffn(xs, ffn2_norm[layer], _prep_gate_up(ffn2_w_gate[layer], ffn2_w_up[layer]),
                  ffn2_w_down[layer].astype(BF16), final_norm if last else None)
    if depth == 0:
        raise NotImplementedError("depth 0 has no FFN to carry the final norm")
    return xs.reshape(b, s, d)
```

```python
import functools

import jax
import jax.numpy as jnp
from jax import lax
from jax.experimental import pallas as pl
from jax.experimental.pallas import tpu as pltpu

F32 = jnp.float32
BF16 = jnp.bfloat16

HEAD = 128
N_HEADS = 16
MIX_WIDTH = N_HEADS * HEAD
MOBA_BLOCK = 256
MOBA_TOPK = 3
RMS_EPS = 1e-6
MACARON_WEIGHT = 0.5
LOG2E = 1.4426950408889634

HGRN_CHUNK = 64
SUBLANES = 8
LANES = 128
HGRN_SUB = 8
HGRN_GROUP = 8
HGRN_SEQ_BLOCK = 1024
HGRN_UNROLL = 4
FF_TILE = 256
FF_PAIR = 2
NORM_ROWS = 16
NORM_UNROLL = 4

V7X_VMEM_BYTES = 64 * 1024 * 1024
VMEM_LIMIT = V7X_VMEM_BYTES - 6 * 1024 * 1024
FFN_VMEM_LIMIT = V7X_VMEM_BYTES - 1 * 1024 * 1024


def _dot(a, b, precision=None):
    return jnp.dot(a, b, preferred_element_type=F32, precision=precision)


def _dot_nt(a, b, precision=None):
    return lax.dot_general(a, b, (((1,), (1,)), ((), ())),
                           preferred_element_type=F32, precision=precision)


def _dot_tn(a, b):
    return lax.dot_general(a, b, (((0,), (0,)), ((), ())), preferred_element_type=F32)


def _grid_step(rank):
    step = pl.program_id(0)
    for axis in range(1, rank):
        step = step * pl.num_programs(axis) + pl.program_id(axis)
    return step


def _cast_blocks(rows, cols, steps):
    for n in range(min(steps, rows // NORM_ROWS), 0, -1):
        if rows % n == 0 and (rows // n) % NORM_ROWS == 0:
            split = 1
            while n * split * 2 <= steps and cols % (split * 2 * LANES) == 0:
                split *= 2
            return rows // n, cols // split
    raise ValueError(f"cannot split {rows} rows into bf16 row-tile blocks")


def _rmsnorm_rows(x_ref, g_ref, o_ref, unroll=NORM_UNROLL):
    groups = x_ref.shape[0] // NORM_ROWS
    gain = g_ref[...]

    def rows(i):
        return pl.ds(pl.multiple_of(i * NORM_ROWS, NORM_ROWS), NORM_ROWS)

    def inv_rms(i):
        x = x_ref[rows(i), :]
        return lax.rsqrt(jnp.mean(x * x, axis=-1, keepdims=True) + RMS_EPS)

    def body(i, r):
        r_next = inv_rms(jnp.minimum(i + 1, groups - 1))
        o_ref[rows(i), :] = ((x_ref[rows(i), :] * r) * gain).astype(o_ref.dtype)
        return r_next

    lax.fori_loop(0, groups, body, inv_rms(0), unroll=unroll)


def _ffn_kernel(*refs, final_norm, n_tiles):
    if final_norm:
        x_hbm, g_ref, wgu_ref, wd_ref, gf_ref, o_ref, x_ref, h_ref, sem = refs
    else:
        x_hbm, g_ref, wgu_ref, wd_ref, o_ref, x_ref, h_ref, sem = refs
    i = pl.program_id(0)
    f = pl.program_id(1)
    last = pl.num_programs(1) - 1
    more_rows = i + 1 < pl.num_programs(0)
    tail = n_tiles - last * FF_PAIR
    slot = i % 2
    tm = x_ref.shape[0]

    def fetch(row_tile):
        return pltpu.make_async_copy(x_hbm.at[pl.ds(row_tile * tm, tm), :], x_ref, sem)

    def tile(k):
        gu = _dot(h_ref[slot], wgu_ref[k])
        gate = gu[:, :FF_TILE]
        up = gu[:, FF_TILE:]
        act = (gate * jax.nn.sigmoid(gate)) * up * MACARON_WEIGHT
        return _dot(act.astype(BF16), wd_ref[k * FF_TILE:(k + 1) * FF_TILE, :])

    @pl.when(f == 0)
    def _():
        @pl.when(i == 0)
        def _():
            fetch(0).start()
            fetch(0).wait()
            _rmsnorm_rows(x_ref, g_ref, h_ref.at[0])

        o_ref[...] = x_ref[...] + tile(0)
        for k in range(1, FF_PAIR):
            o_ref[...] += tile(k)

    @pl.when(jnp.logical_and(f == 1, more_rows))
    def _():
        fetch(i + 1).start()

    @pl.when(jnp.logical_and(f > 0, f < last))
    def _():
        for k in range(FF_PAIR):
            o_ref[...] += tile(k)

    @pl.when(f == last)
    def _():
        @pl.when(more_rows)
        def _():
            fetch(i + 1).wait()

        _rmsnorm_rows(x_ref, g_ref, h_ref.at[1 - slot], unroll=True)
        for k in range(tail):
            o_ref[...] += tile(k)
        if final_norm:
            _rmsnorm_rows(o_ref, gf_ref, o_ref)


def _ffn(x, gain, wgu, wd, final_gain=None, *, tm=512):
    t, d = x.shape
    nf = wgu.shape[0]
    steps = pl.cdiv(nf, FF_PAIR)
    assert steps >= 2 and t % tm == 0
    final_norm = final_gain is not None
    in_specs = [
        pl.BlockSpec(memory_space=pl.ANY),
        pl.BlockSpec((1, d), lambda i, f: (0, 0)),
        pl.BlockSpec((FF_PAIR, d, 2 * FF_TILE), lambda i, f: (f, 0, 0)),
        pl.BlockSpec((FF_PAIR * FF_TILE, d), lambda i, f: (f, 0)),
    ]
    args = [x, gain.reshape(1, d), wgu, wd]
    if final_norm:
        in_specs.append(pl.BlockSpec((1, d), lambda i, f: (0, 0)))
        args.append(final_gain.reshape(1, d))
    return pl.pallas_call(
        functools.partial(_ffn_kernel, final_norm=final_norm, n_tiles=nf),
        grid=(t // tm, steps),
        in_specs=in_specs,
        out_specs=pl.BlockSpec((tm, d), lambda i, f: (i, 0)),
        out_shape=jax.ShapeDtypeStruct((t, d), F32),
        scratch_shapes=[
            pltpu.VMEM((tm, d), F32),
            pltpu.VMEM((2, tm, d), BF16),
            pltpu.SemaphoreType.DMA(()),
        ],
        compiler_params=pltpu.CompilerParams(
            dimension_semantics=("arbitrary", "arbitrary"),
            vmem_limit_bytes=FFN_VMEM_LIMIT),
        name="ffn_final" if final_norm else "ffn",
    )(*args)


def _pack_gate_up_kernel(wg_ref, wu_ref, o_ref):
    o_ref[:, :FF_TILE] = wg_ref[...].astype(BF16)
    o_ref[:, FF_TILE:] = wu_ref[...].astype(BF16)


def _prep_gate_up(w_gate, w_up):
    d, d_ff = w_gate.shape
    nf = d_ff // FF_TILE
    col = pl.BlockSpec((d, FF_TILE), lambda f: (0, f))
    return pl.pallas_call(
        _pack_gate_up_kernel,
        grid=(nf,),
        in_specs=[col, col],
        out_specs=pl.BlockSpec((None, d, 2 * FF_TILE), lambda f: (f, 0, 0)),
        out_shape=jax.ShapeDtypeStruct((nf, d, 2 * FF_TILE), BF16),
        compiler_params=pltpu.CompilerParams(
            dimension_semantics=("parallel",), vmem_limit_bytes=VMEM_LIMIT),
        name="pack_gate_up",
    )(w_gate, w_up)


def _in_proj_kernel(x_hbm, g_ref, w_ref, lb_ref, o_ref, x_ref, h_ref, sem, *, tiles_per_part):
    i = pl.program_id(0)
    j = pl.program_id(1)
    part = j // tiles_per_part
    tm = x_ref.shape[0]

    def fetch(tile):
        return pltpu.make_async_copy(x_hbm.at[pl.ds(tile * tm, tm), :], x_ref, sem)

    @pl.when(j == 0)
    def _():
        @pl.when(i == 0)
        def _():
            fetch(0).start()

        fetch(i).wait()
        _rmsnorm_rows(x_ref, g_ref, h_ref)

    @pl.when(jnp.logical_and(j == 1, i + 1 < pl.num_programs(0)))
    def _():
        fetch(i + 1).start()

    @pl.when(jnp.logical_or(part == 0, part == 3))
    def _():
        z = _dot(h_ref[...], w_ref[...])
        o_ref[...] = (z * jax.nn.sigmoid(z)) * jnp.where(part == 0, HEAD ** -0.5, 1.0)

    @pl.when(part == 1)
    def _():
        lb = lb_ref[...]
        o_ref[...] = lb + (1.0 - lb) * jax.nn.sigmoid(_dot(h_ref[...], w_ref[...]))

    @pl.when(jnp.logical_or(part == 2, part > 3))
    def _():
        o_ref[...] = _dot(h_ref[...], w_ref[...])


def _in_proj(x, gain, w, lower_bound, *, tm=1024, tn=1024):
    t, d = x.shape
    n = w.shape[1]
    tiles_per_part = MIX_WIDTH // tn
    assert n // tn >= 2 and t % tm == 0
    return pl.pallas_call(
        functools.partial(_in_proj_kernel, tiles_per_part=tiles_per_part),
        grid=(t // tm, n // tn),
        in_specs=[
            pl.BlockSpec(memory_space=pl.ANY),
            pl.BlockSpec((1, d), lambda i, j: (0, 0)),
            pl.BlockSpec((d, tn), lambda i, j: (0, j)),
            pl.BlockSpec((1, tn),
                         lambda i, j: (0, jnp.clip(j - tiles_per_part, 0, tiles_per_part - 1))),
        ],
        out_specs=pl.BlockSpec((tm, tn), lambda i, j: (i, j)),
        out_shape=jax.ShapeDtypeStruct((t, n), F32),
        scratch_shapes=[
            pltpu.VMEM((tm, d), F32),
            pltpu.VMEM((tm, d), BF16),
            pltpu.SemaphoreType.DMA(()),
        ],
        compiler_params=pltpu.CompilerParams(
            dimension_semantics=("arbitrary", "arbitrary"),
            vmem_limit_bytes=VMEM_LIMIT),
        name="in_proj",
    )(x, gain.reshape(1, d), w, lower_bound.reshape(1, MIX_WIDTH))


def _out_proj_kernel(oh_ref, om_ref, wh_ref, wm_ref, x_ref, o_ref):
    o_ref[...] = x_ref[...] + _dot(oh_ref[...], wh_ref[...]) + _dot(om_ref[...], wm_ref[...])


def _out_proj(o_hgrn, o_moba, w, x, *, tm=1024, tn=1024):
    t, d = x.shape
    kh = o_hgrn.shape[1]
    return pl.pallas_call(
        _out_proj_kernel,
        grid=(t // tm, d // tn),
        in_specs=[
            pl.BlockSpec((tm, kh), lambda i, j: (i, 0)),
            pl.BlockSpec((tm, kh), lambda i, j: (i, 0)),
            pl.BlockSpec((kh, tn), lambda i, j: (0, j)),
            pl.BlockSpec((kh, tn), lambda i, j: (1, j)),
            pl.BlockSpec((tm, tn), lambda i, j: (i, j)),
        ],
        out_specs=pl.BlockSpec((tm, tn), lambda i, j: (i, j)),
        out_shape=jax.ShapeDtypeStruct((t, d), F32),
        compiler_params=pltpu.CompilerParams(
            dimension_semantics=("parallel", "arbitrary"),
            vmem_limit_bytes=VMEM_LIMIT),
        name="out_proj",
    )(o_hgrn, o_moba, w, w, x)


def _hgrn_kernel(q_ref, f_ref, v_ref, gate_ref, gain_ref, w_ref, o_ref, wb_ref,
                 st_ref, cum_ref, key_ref, *, n_cast):
    c, sub, half = HGRN_CHUNK, HGRN_SUB, SUBLANES
    rows_total = q_ref.shape[0]
    gain = gain_ref[...]
    tri = (lax.broadcasted_iota(jnp.int32, (c, c), 0)
           >= lax.broadcasted_iota(jnp.int32, (c, c), 1)).astype(BF16)
    lane = lax.broadcasted_iota(jnp.int32, (half, c), 1)
    trow = lax.broadcasted_iota(jnp.int32, (half, HEAD), 0)

    @pl.when(_grid_step(3) < n_cast)
    def _():
        wb_ref[...] = w_ref[...].astype(BF16)

    @pl.when(pl.program_id(2) == 0)
    def _():
        st_ref[...] = jnp.zeros_like(st_ref)

    def chunk(ci, carry):
        sl = pl.ds(pl.multiple_of(ci * c, c), c)
        forget_all = f_ref[sl, :]
        logf = jnp.log2(forget_all)
        hi = logf.astype(BF16)
        rest = logf - hi.astype(F32)
        mid = rest.astype(BF16)
        lo = (rest - mid.astype(F32)).astype(BF16)
        cum_all = _dot(tri, hi) + _dot(tri, mid) + _dot(tri, lo)
        key_all = cum_all - jnp.log2(1.0 - forget_all)

        heads = range(HGRN_GROUP)
        cols = [slice(g * HEAD, (g + 1) * HEAD) for g in heads]
        cum = [cum_all[:, cols[g]] for g in heads]
        key = [key_all[:, cols[g]] for g in heads]
        q, vb, o = [], [], []
        for g in heads:
            q.append(q_ref[sl, cols[g]])
            vb.append(v_ref[sl, cols[g]].astype(BF16))
            cum_ref[g] = cum[g]
            key_ref[g] = key[g]
        for g in heads:
            last = cum[g][c - 1:c, :]
            st = st_ref[g]
            o.append(_dot_nt((q[g] * jnp.exp2(cum[g])).astype(BF16), st.astype(BF16)))
            k_end = jnp.exp2(last - key[g]).astype(BF16)
            st_ref[g] = st * jnp.exp2(last) + _dot_tn(vb[g], k_end)

        off = {}
        for i in range(1, c // sub):
            lo_row = i * sub
            for g in heads:
                edge = cum_ref[g, lo_row - 1:lo_row, :]
                q_t = (q[g][lo_row:lo_row + sub, :]
                       * jnp.exp2(cum[g][lo_row:lo_row + sub, :] - edge)).astype(BF16)
                k_t = jnp.exp2(edge - key[g][:lo_row, :])
                k_t = jnp.concatenate([k_t, jnp.zeros((c - lo_row, HEAD), F32)], axis=0)
                off[g, i] = _dot_nt(q_t, k_t.astype(BF16))

        blocks = [[] for _ in heads]
        for i in range(c // sub):
            lo_row = i * sub
            for h in range(sub // half):
                top = lo_row + h * half
                for g in heads:
                    cum_b = cum[g][top:top + half, :]
                    q_b = q[g][top:top + half, :]
                    a = jnp.zeros((half, c), F32)
                    for s in range((h + 1) * half):
                        r = lo_row + s
                        diff = cum_b - key_ref[g, r:r + 1, :]
                        if s > h * half:
                            diff = jnp.where(trow >= s - h * half, diff, -jnp.inf)
                        p = q_b * jnp.exp2(diff)
                        a = jnp.where(lane == r, jnp.sum(p, axis=-1, keepdims=True), a)
                    if i > 0:
                        a = a + off[g, i][h * half:(h + 1) * half, :]
                    blocks[g].append(a)

        for g in heads:
            scores = jnp.concatenate(blocks[g], axis=0)
            og = o[g] + _dot(scores.astype(BF16), vb[g])
            r = lax.rsqrt(jnp.mean(og * og, axis=-1, keepdims=True) + RMS_EPS)
            o_ref[sl, cols[g]] = (((og * r) * gain) * gate_ref[sl, cols[g]]).astype(o_ref.dtype)
        return carry

    lax.fori_loop(0, rows_total // c, chunk, 0, unroll=HGRN_UNROLL)


def _hgrn(proj, norm_gain, w_cast):
    b, s, _ = proj.shape
    grp = HGRN_GROUP
    ng = N_HEADS // grp
    sb = min(HGRN_SEQ_BLOCK, s)
    assert s % sb == 0 and sb % (HGRN_CHUNK * HGRN_UNROLL) == 0
    grid = (b, ng, s // sb)
    rows, cols = w_cast.shape
    blk_rows, blk_cols = _cast_blocks(rows, cols, grid[0] * grid[1] * grid[2])
    col_blocks = cols // blk_cols
    n_cast = (rows // blk_rows) * col_blocks
    col = lambda off: pl.BlockSpec((None, sb, grp * HEAD), lambda bi, gi, si: (bi, si, off + gi))

    def cast_index(bi, gi, si):
        blk = jnp.minimum((bi * grid[1] + gi) * grid[2] + si, n_cast - 1)
        return blk // col_blocks, blk % col_blocks

    cast_blk = pl.BlockSpec((blk_rows, blk_cols), cast_index)
    return pl.pallas_call(
        functools.partial(_hgrn_kernel, n_cast=n_cast),
        grid=grid,
        in_specs=[
            col(0), col(ng), col(2 * ng), col(3 * ng),
            pl.BlockSpec((1, HEAD), lambda bi, gi, si: (0, 0)),
            cast_blk,
        ],
        out_specs=[pl.BlockSpec((None, sb, grp * HEAD), lambda bi, gi, si: (bi, si, gi)), cast_blk],
        out_shape=[jax.ShapeDtypeStruct((b, s, MIX_WIDTH), BF16),
                   jax.ShapeDtypeStruct((rows, cols), BF16)],
        scratch_shapes=[
            pltpu.VMEM((grp, HEAD, HEAD), F32),
            pltpu.VMEM((grp, HGRN_CHUNK, HEAD), F32),
            pltpu.VMEM((grp, HGRN_CHUNK, HEAD), F32),
        ],
        compiler_params=pltpu.CompilerParams(
            dimension_semantics=("arbitrary", "arbitrary", "arbitrary"),
            vmem_limit_bytes=VMEM_LIMIT),
        name="hgrn2",
    )(proj, proj, proj, proj, norm_gain.reshape(1, HEAD), w_cast)


def _split3(v):
    hi = v.astype(BF16).astype(F32)
    mid = (v - hi).astype(BF16).astype(F32)
    lo = ((v - hi) - mid).astype(BF16).astype(F32)
    return hi, mid, lo


def _moba_kernel(slopes_ref, units_ref, q_ref, k_ref, v_ref, wg_ref, wu_ref, o_ref, pack_ref,
                 kb_ref, vt_ref, kmean_ref, *, n_pack):
    blk = MOBA_BLOCK
    seq = q_ref.shape[0]
    nb = seq // blk
    topk = min(MOBA_TOPK, max(nb - 1, 1))
    head = pl.program_id(1)
    to_log2 = (HEAD ** -0.5) * LOG2E
    slope2 = slopes_ref[head] * LOG2E
    unit = units_ref[head]
    neg_inf = jnp.full((1, blk), -jnp.inf, F32)

    k_pos = lax.broadcasted_iota(jnp.int32, (blk, HEAD), 0).astype(F32) * unit
    k_lane = lax.broadcasted_iota(jnp.int32, (blk, HEAD), 1)
    hi, mid, lo = _split3(k_pos)
    aug_k = jnp.where(k_lane == 0, hi, jnp.where(k_lane == 1, mid, jnp.where(
        k_lane == 2, lo, jnp.where(k_lane < 6, 1.0, 0.0)))).astype(BF16)
    q_pos = lax.broadcasted_iota(jnp.int32, (HEAD, blk), 1).astype(F32) * (-unit)
    q_row = lax.broadcasted_iota(jnp.int32, (HEAD, blk), 0)
    hi, mid, lo = _split3(q_pos)
    aug_q = jnp.where(q_row < 3, 1.0, jnp.where(q_row == 3, hi, jnp.where(
        q_row == 4, mid, jnp.where(q_row == 5, lo, 0.0)))).astype(BF16)

    @pl.when(_grid_step(2) < n_pack)
    def _():
        _pack_gate_up_kernel(wg_ref, wu_ref, pack_ref)

    kmean_ref[...] = jnp.zeros_like(kmean_ref)
    for n in range(nb):
        rows = slice(n * blk, (n + 1) * blk)
        kblk = k_ref[rows, :]
        kmean_ref[n:n + 1, :] = jnp.mean(kblk, axis=0, keepdims=True)
        kb_ref[rows, :HEAD] = kblk.astype(BF16)
        kb_ref[rows, HEAD:] = aug_k
        vt_ref[:, rows] = v_ref[rows, :].T.astype(BF16)

    causal = (lax.broadcasted_iota(jnp.int32, (blk, blk), 0)
              <= lax.broadcasted_iota(jnp.int32, (blk, blk), 1))

    def score(qb):
        qsl = slice(qb * blk, (qb + 1) * blk)
        qt = q_ref[qsl, :].T
        if qb <= topk:
            picked = [None] * qb
        else:
            rows_p = -(-qb // 8) * 8
            blk_row = lax.broadcasted_iota(jnp.int32, (rows_p, blk), 0)
            gate = _dot(kmean_ref[0:rows_p, :], qt, precision=lax.Precision.HIGHEST)
            gate = jnp.where(blk_row < qb, gate, -jnp.inf)
            sel = jnp.zeros((rows_p, blk), F32)
            for _ in range(topk):
                best = jnp.max(gate, axis=0, keepdims=True)
                first = jnp.min(jnp.where(gate == best, blk_row, rows_p), axis=0, keepdims=True)
                pick = blk_row == first
                sel = jnp.where(pick, 1.0, sel)
                gate = jnp.where(pick, -jnp.inf, gate)
            picked = [sel[n:n + 1, :] > 0.0 for n in range(qb)]

        nk = (qb + 1) * blk
        qt_aug = jnp.concatenate([qt.astype(BF16), aug_q], axis=0)
        raw = _dot(kb_ref[0:nk, :], qt_aug)
        return raw, picked

    def tile_max(qb, raw, picked):
        nk = (qb + 1) * blk
        tiles, shifts = [], []
        for n in range(qb):
            tiles.append(raw[n * blk:(n + 1) * blk, :] * to_log2)
            offset = slope2 * float((n - qb) * blk)
            shifts.append(offset if picked[n] is None else jnp.where(picked[n], offset, neg_inf))
        tiles.append(jnp.where(causal, raw[qb * blk:nk, :] * to_log2, -jnp.inf))
        shifts.append(0.0)
        m = None
        for t, sh in zip(tiles, shifts):
            top = jnp.max(t, axis=0, keepdims=True) + sh
            m = top if m is None else jnp.maximum(m, top)
        return tiles, [sh - m for sh in shifts]

    def softmax(qb, tiles, shifts):
        probs, denom = [], None
        for t, sh in zip(tiles, shifts):
            p = jnp.exp2(t + sh)
            probs.append(p.astype(BF16))
            part = jnp.sum(p, axis=0, keepdims=True)
            denom = part if denom is None else denom + part
        return jnp.concatenate(probs, axis=0), denom

    def output(qb, probs, denom):
        nk = (qb + 1) * blk
        acc = _dot(vt_ref[:, 0:nk], probs)
        o_ref[qb * blk:nk, :] = (acc / denom).T.astype(o_ref.dtype)

    stages = [score, tile_max, softmax, output]
    live = {}
    for step in range(nb + len(stages) - 1):
        for k, stage in enumerate(stages):
            qb = step - k
            if 0 <= qb < nb:
                args = live.pop((qb, k - 1)) if k else ()
                live[qb, k] = stage(qb, *args)


def _moba(proj, w_gate, w_up):
    b, s, _ = proj.shape
    assert s % MOBA_BLOCK == 0
    nb = s // MOBA_BLOCK
    nbp = -(-nb // 8) * 8
    d, d_ff = w_gate.shape
    n_pack = d_ff // FF_TILE
    assert n_pack <= b * N_HEADS
    slopes = jnp.exp2(-8.0 * jnp.arange(1, N_HEADS + 1, dtype=F32) / N_HEADS)
    col = lambda off: pl.BlockSpec((None, s, HEAD), lambda bi, hi: (bi, 0, off + hi))
    tile_of = lambda bi, hi: jnp.minimum(bi * N_HEADS + hi, n_pack - 1)
    w_col = pl.BlockSpec((d, FF_TILE), lambda bi, hi: (0, tile_of(bi, hi)))
    return pl.pallas_call(
        functools.partial(_moba_kernel, n_pack=n_pack),
        grid=(b, N_HEADS),
        in_specs=[
            pl.BlockSpec(memory_space=pltpu.SMEM),
            pl.BlockSpec(memory_space=pltpu.SMEM),
            col(4 * N_HEADS), col(5 * N_HEADS), col(6 * N_HEADS),
            w_col, w_col,
        ],
        out_specs=[
            pl.BlockSpec((None, s, HEAD), lambda bi, hi: (bi, 0, hi)),
            pl.BlockSpec((None, d, 2 * FF_TILE), lambda bi, hi: (tile_of(bi, hi), 0, 0)),
        ],
        out_shape=[jax.ShapeDtypeStruct((b, s, MIX_WIDTH), BF16),
                   jax.ShapeDtypeStruct((n_pack, d, 2 * FF_TILE), BF16)],
        scratch_shapes=[
            pltpu.VMEM((s, 2 * HEAD), BF16),
            pltpu.VMEM((HEAD, s), BF16),
            pltpu.VMEM((nbp, HEAD), F32),
        ],
        compiler_params=pltpu.CompilerParams(
            dimension_semantics=("arbitrary", "arbitrary"),
            vmem_limit_bytes=VMEM_LIMIT),
        name="moba",
    )(slopes, slopes * HEAD ** 0.5, proj, proj, proj, w_gate, w_up)


def kernel(x, ffn1_norm, ffn1_w_gate, ffn1_w_up, ffn1_w_down, mix_norm, w_in, hgrn_lower_bounds, hgrn_out_norm, w_out, ffn2_norm, ffn2_w_gate, ffn2_w_up, ffn2_w_down, final_norm):
    b, s, d = x.shape
    depth = ffn1_norm.shape[0]
    lb_all = jnp.cumsum(jax.nn.softmax(hgrn_lower_bounds.astype(F32), axis=0), axis=0)
    xs = x.reshape(b * s, d)
    for layer in range(depth):
        xs = _ffn(xs, ffn1_norm[layer], _prep_gate_up(ffn1_w_gate[layer], ffn1_w_up[layer]),
                  ffn1_w_down[layer].astype(BF16))
        proj = _in_proj(xs, mix_norm[layer], w_in[layer].astype(BF16), lb_all[layer])
        proj = proj.reshape(b, s, -1)
        o_hgrn, wd2 = _hgrn(proj, hgrn_out_norm[layer], ffn2_w_down[layer])
        o_moba, wgu2 = _moba(proj, ffn2_w_gate[layer], ffn2_w_up[layer])
        xs = _out_proj(o_hgrn.reshape(b * s, MIX_WIDTH), o_moba.reshape(b * s, MIX_WIDTH),
                       w_out[layer].astype(BF16), xs)
        last = layer == depth - 1
        xs = _ffn(xs, ffn2_norm[layer], wgu2, wd2, final_norm if last else None)
    if depth == 0:
        raise NotImplementedError("depth 0 has no FFN to carry the final norm")
    return xs.reshape(b, s, d)
```

```python
import functools

import jax
import jax.numpy as jnp
from jax import lax
from jax.experimental import pallas as pl
from jax.experimental.pallas import tpu as pltpu

F32 = jnp.float32
BF16 = jnp.bfloat16

HEAD = 128
N_HEADS = 16
MIX_WIDTH = N_HEADS * HEAD
MOBA_BLOCK = 256
MOBA_TOPK = 3
RMS_EPS = 1e-6
MACARON_WEIGHT = 0.5
LOG2E = 1.4426950408889634

HGRN_CHUNK = 64
SUBLANES = 8
LANES = 128
HGRN_SUB = 8
HGRN_GROUP = 8
HGRN_SEQ_BLOCK = 1024
HGRN_UNROLL = 4
FF_TILE = 256
FF_PAIR = 2
NORM_ROWS = 16
NORM_UNROLL = 4

V7X_VMEM_BYTES = 64 * 1024 * 1024
VMEM_LIMIT = V7X_VMEM_BYTES - 6 * 1024 * 1024
FFN_VMEM_LIMIT = V7X_VMEM_BYTES - 1 * 1024 * 1024


def _dot(a, b, precision=None):
    return jnp.dot(a, b, preferred_element_type=F32, precision=precision)


def _dot_nt(a, b, precision=None):
    return lax.dot_general(a, b, (((1,), (1,)), ((), ())),
                           preferred_element_type=F32, precision=precision)


def _dot_tn(a, b):
    return lax.dot_general(a, b, (((0,), (0,)), ((), ())), preferred_element_type=F32)


def _grid_step(rank):
    step = pl.program_id(0)
    for axis in range(1, rank):
        step = step * pl.num_programs(axis) + pl.program_id(axis)
    return step


def _cast_blocks(rows, cols, steps):
    for n in range(min(steps, rows // NORM_ROWS), 0, -1):
        if rows % n == 0 and (rows // n) % NORM_ROWS == 0:
            split = 1
            while n * split * 2 <= steps and cols % (split * 2 * LANES) == 0:
                split *= 2
            return rows // n, cols // split
    raise ValueError(f"cannot split {rows} rows into bf16 row-tile blocks")


def _hosted_cast_spec(w, steps, step_of):
    rows, cols = w.shape
    blk_rows, blk_cols = _cast_blocks(rows, cols, steps)
    col_blocks = cols // blk_cols
    n_cast = (rows // blk_rows) * col_blocks

    def index(*grid_idx):
        blk = jnp.minimum(step_of(*grid_idx), n_cast - 1)
        return blk // col_blocks, blk % col_blocks

    return pl.BlockSpec((blk_rows, blk_cols), index), n_cast


def _rmsnorm_rows(x_ref, g_ref, o_ref, unroll=NORM_UNROLL):
    groups = x_ref.shape[0] // NORM_ROWS
    gain = g_ref[...]

    def rows(i):
        return pl.ds(pl.multiple_of(i * NORM_ROWS, NORM_ROWS), NORM_ROWS)

    def inv_rms(i):
        x = x_ref[rows(i), :]
        return lax.rsqrt(jnp.mean(x * x, axis=-1, keepdims=True) + RMS_EPS)

    def body(i, r):
        r_next = inv_rms(jnp.minimum(i + 1, groups - 1))
        o_ref[rows(i), :] = ((x_ref[rows(i), :] * r) * gain).astype(o_ref.dtype)
        return r_next

    lax.fori_loop(0, groups, body, inv_rms(0), unroll=unroll)


def _ffn_kernel(*refs, final_norm, n_tiles, n_cast):
    refs = list(refs)
    x_hbm, g_ref, wgu_ref, wd_ref = refs[:4]
    del refs[:4]
    gf_ref = refs.pop(0) if final_norm else None
    w_ref = refs.pop(0) if n_cast else None
    o_ref = refs.pop(0)
    wb_ref = refs.pop(0) if n_cast else None
    x_ref, h_ref, sem = refs
    i = pl.program_id(0)
    f = pl.program_id(1)
    last = pl.num_programs(1) - 1
    more_rows = i + 1 < pl.num_programs(0)
    tail = n_tiles - last * FF_PAIR
    slot = i % 2
    tm = x_ref.shape[0]

    def fetch(row_tile):
        return pltpu.make_async_copy(x_hbm.at[pl.ds(row_tile * tm, tm), :], x_ref, sem)

    if n_cast:
        @pl.when(_grid_step(2) < n_cast)
        def _():
            wb_ref[...] = w_ref[...].astype(BF16)

    def tile(k):
        gu = _dot(h_ref[slot], wgu_ref[k])
        gate = gu[:, :FF_TILE]
        up = gu[:, FF_TILE:]
        act = (gate * jax.nn.sigmoid(gate)) * up * MACARON_WEIGHT
        return _dot(act.astype(BF16), wd_ref[k * FF_TILE:(k + 1) * FF_TILE, :])

    @pl.when(f == 0)
    def _():
        @pl.when(i == 0)
        def _():
            fetch(0).start()
            fetch(0).wait()
            _rmsnorm_rows(x_ref, g_ref, h_ref.at[0])

        o_ref[...] = x_ref[...] + tile(0)
        for k in range(1, FF_PAIR):
            o_ref[...] += tile(k)

    @pl.when(jnp.logical_and(f == 1, more_rows))
    def _():
        fetch(i + 1).start()

    @pl.when(jnp.logical_and(f > 0, f < last))
    def _():
        for k in range(FF_PAIR):
            o_ref[...] += tile(k)

    @pl.when(f == last)
    def _():
        @pl.when(more_rows)
        def _():
            fetch(i + 1).wait()

        _rmsnorm_rows(x_ref, g_ref, h_ref.at[1 - slot], unroll=True)
        for k in range(tail):
            o_ref[...] += tile(k)
        if final_norm:
            _rmsnorm_rows(o_ref, gf_ref, o_ref)


def _ffn(x, gain, wgu, wd, final_gain=None, w_cast=None, *, tm=512):
    t, d = x.shape
    nf = wgu.shape[0]
    steps = pl.cdiv(nf, FF_PAIR)
    assert steps >= 2 and t % tm == 0
    final_norm = final_gain is not None
    in_specs = [
        pl.BlockSpec(memory_space=pl.ANY),
        pl.BlockSpec((1, d), lambda i, f: (0, 0)),
        pl.BlockSpec((FF_PAIR, d, 2 * FF_TILE), lambda i, f: (f, 0, 0)),
        pl.BlockSpec((FF_PAIR * FF_TILE, d), lambda i, f: (f, 0)),
    ]
    args = [x, gain.reshape(1, d), wgu, wd]
    if final_norm:
        in_specs.append(pl.BlockSpec((1, d), lambda i, f: (0, 0)))
        args.append(final_gain.reshape(1, d))
    out_specs = [pl.BlockSpec((tm, d), lambda i, f: (i, 0))]
    out_shape = [jax.ShapeDtypeStruct((t, d), F32)]
    n_cast = 0
    if w_cast is not None:
        cast_blk, n_cast = _hosted_cast_spec(w_cast, (t // tm) * steps, lambda i, f: i * steps + f)
        in_specs.append(cast_blk)
        args.append(w_cast)
        out_specs.append(cast_blk)
        out_shape.append(jax.ShapeDtypeStruct(w_cast.shape, BF16))
    return pl.pallas_call(
        functools.partial(_ffn_kernel, final_norm=final_norm, n_tiles=nf, n_cast=n_cast),
        grid=(t // tm, steps),
        in_specs=in_specs,
        out_specs=out_specs,
        out_shape=out_shape,
        scratch_shapes=[
            pltpu.VMEM((tm, d), F32),
            pltpu.VMEM((2, tm, d), BF16),
            pltpu.SemaphoreType.DMA(()),
        ],
        compiler_params=pltpu.CompilerParams(
            dimension_semantics=("arbitrary", "arbitrary"),
            vmem_limit_bytes=FFN_VMEM_LIMIT),
        name="ffn_final" if final_norm else "ffn",
    )(*args)


def _pack_gate_up_kernel(wg_ref, wu_ref, o_ref):
    o_ref[:, :FF_TILE] = wg_ref[...].astype(BF16)
    o_ref[:, FF_TILE:] = wu_ref[...].astype(BF16)


def _prep_gate_up(w_gate, w_up):
    d, d_ff = w_gate.shape
    nf = d_ff // FF_TILE
    col = pl.BlockSpec((d, FF_TILE), lambda f: (0, f))
    return pl.pallas_call(
        _pack_gate_up_kernel,
        grid=(nf,),
        in_specs=[col, col],
        out_specs=pl.BlockSpec((None, d, 2 * FF_TILE), lambda f: (f, 0, 0)),
        out_shape=jax.ShapeDtypeStruct((nf, d, 2 * FF_TILE), BF16),
        compiler_params=pltpu.CompilerParams(
            dimension_semantics=("parallel",), vmem_limit_bytes=VMEM_LIMIT),
        name="pack_gate_up",
    )(w_gate, w_up)


def _in_proj_kernel(x_hbm, g_ref, w_ref, lb_ref, o_ref, x_ref, h_ref, sem, *, tiles_per_part):
    i = pl.program_id(0)
    j = pl.program_id(1)
    part = j // tiles_per_part
    tm = x_ref.shape[0]

    def fetch(tile):
        return pltpu.make_async_copy(x_hbm.at[pl.ds(tile * tm, tm), :], x_ref, sem)

    @pl.when(j == 0)
    def _():
        @pl.when(i == 0)
        def _():
            fetch(0).start()

        fetch(i).wait()
        _rmsnorm_rows(x_ref, g_ref, h_ref)

    @pl.when(jnp.logical_and(j == 1, i + 1 < pl.num_programs(0)))
    def _():
        fetch(i + 1).start()

    @pl.when(jnp.logical_or(part == 0, part == 3))
    def _():
        z = _dot(h_ref[...], w_ref[...])
        o_ref[...] = (z * jax.nn.sigmoid(z)) * jnp.where(part == 0, HEAD ** -0.5, 1.0)

    @pl.when(part == 1)
    def _():
        lb = lb_ref[...]
        o_ref[...] = lb + (1.0 - lb) * jax.nn.sigmoid(_dot(h_ref[...], w_ref[...]))

    @pl.when(jnp.logical_or(part == 2, part > 3))
    def _():
        o_ref[...] = _dot(h_ref[...], w_ref[...])


def _in_proj(x, gain, w, lower_bound, *, tm=1024, tn=1024):
    t, d = x.shape
    n = w.shape[1]
    tiles_per_part = MIX_WIDTH // tn
    assert n // tn >= 2 and t % tm == 0
    return pl.pallas_call(
        functools.partial(_in_proj_kernel, tiles_per_part=tiles_per_part),
        grid=(t // tm, n // tn),
        in_specs=[
            pl.BlockSpec(memory_space=pl.ANY),
            pl.BlockSpec((1, d), lambda i, j: (0, 0)),
            pl.BlockSpec((d, tn), lambda i, j: (0, j)),
            pl.BlockSpec((1, tn),
                         lambda i, j: (0, jnp.clip(j - tiles_per_part, 0, tiles_per_part - 1))),
        ],
        out_specs=pl.BlockSpec((tm, tn), lambda i, j: (i, j)),
        out_shape=jax.ShapeDtypeStruct((t, n), F32),
        scratch_shapes=[
            pltpu.VMEM((tm, d), F32),
            pltpu.VMEM((tm, d), BF16),
            pltpu.SemaphoreType.DMA(()),
        ],
        compiler_params=pltpu.CompilerParams(
            dimension_semantics=("arbitrary", "arbitrary"),
            vmem_limit_bytes=VMEM_LIMIT),
        name="in_proj",
    )(x, gain.reshape(1, d), w, lower_bound.reshape(1, MIX_WIDTH))


def _out_proj_kernel(oh_ref, om_ref, wh_ref, wm_ref, x_ref, o_ref):
    o_ref[...] = x_ref[...] + _dot(oh_ref[...], wh_ref[...]) + _dot(om_ref[...], wm_ref[...])


def _out_proj(o_hgrn, o_moba, w, x, *, tm=1024, tn=1024):
    t, d = x.shape
    kh = o_hgrn.shape[1]
    return pl.pallas_call(
        _out_proj_kernel,
        grid=(t // tm, d // tn),
        in_specs=[
            pl.BlockSpec((tm, kh), lambda i, j: (i, 0)),
            pl.BlockSpec((tm, kh), lambda i, j: (i, 0)),
            pl.BlockSpec((kh, tn), lambda i, j: (0, j)),
            pl.BlockSpec((kh, tn), lambda i, j: (1, j)),
            pl.BlockSpec((tm, tn), lambda i, j: (i, j)),
        ],
        out_specs=pl.BlockSpec((tm, tn), lambda i, j: (i, j)),
        out_shape=jax.ShapeDtypeStruct((t, d), F32),
        compiler_params=pltpu.CompilerParams(
            dimension_semantics=("parallel", "arbitrary"),
            vmem_limit_bytes=VMEM_LIMIT),
        name="out_proj",
    )(o_hgrn, o_moba, w, w, x)


def _hgrn_kernel(q_ref, f_ref, v_ref, gate_ref, gain_ref, w_ref, o_ref, wb_ref,
                 st_ref, cum_ref, key_ref, *, n_cast):
    c, sub, half = HGRN_CHUNK, HGRN_SUB, SUBLANES
    rows_total = q_ref.shape[0]
    gain = gain_ref[...]
    tri = (lax.broadcasted_iota(jnp.int32, (c, c), 0)
           >= lax.broadcasted_iota(jnp.int32, (c, c), 1)).astype(BF16)
    lane = lax.broadcasted_iota(jnp.int32, (half, c), 1)
    trow = lax.broadcasted_iota(jnp.int32, (half, HEAD), 0)

    @pl.when(_grid_step(3) < n_cast)
    def _():
        wb_ref[...] = w_ref[...].astype(BF16)

    @pl.when(pl.program_id(2) == 0)
    def _():
        st_ref[...] = jnp.zeros_like(st_ref)

    def chunk(ci, carry):
        sl = pl.ds(pl.multiple_of(ci * c, c), c)
        forget_all = f_ref[sl, :]
        logf = jnp.log2(forget_all)
        hi = logf.astype(BF16)
        rest = logf - hi.astype(F32)
        mid = rest.astype(BF16)
        lo = (rest - mid.astype(F32)).astype(BF16)
        cum_all = _dot(tri, hi) + _dot(tri, mid) + _dot(tri, lo)
        key_all = cum_all - jnp.log2(1.0 - forget_all)

        heads = range(HGRN_GROUP)
        cols = [slice(g * HEAD, (g + 1) * HEAD) for g in heads]
        cum = [cum_all[:, cols[g]] for g in heads]
        key = [key_all[:, cols[g]] for g in heads]
        q, vb, o = [], [], []
        for g in heads:
            q.append(q_ref[sl, cols[g]])
            vb.append(v_ref[sl, cols[g]].astype(BF16))
            cum_ref[g] = cum[g]
            key_ref[g] = key[g]
        for g in heads:
            last = cum[g][c - 1:c, :]
            st = st_ref[g]
            o.append(_dot_nt((q[g] * jnp.exp2(cum[g])).astype(BF16), st.astype(BF16)))
            k_end = jnp.exp2(last - key[g]).astype(BF16)
            st_ref[g] = st * jnp.exp2(last) + _dot_tn(vb[g], k_end)

        off = {}
        for i in range(1, c // sub):
            lo_row = i * sub
            for g in heads:
                edge = cum_ref[g, lo_row - 1:lo_row, :]
                q_t = (q[g][lo_row:lo_row + sub, :]
                       * jnp.exp2(cum[g][lo_row:lo_row + sub, :] - edge)).astype(BF16)
                k_t = jnp.exp2(edge - key[g][:lo_row, :])
                k_t = jnp.concatenate([k_t, jnp.zeros((c - lo_row, HEAD), F32)], axis=0)
                off[g, i] = _dot_nt(q_t, k_t.astype(BF16))

        blocks = [[] for _ in heads]
        for i in range(c // sub):
            lo_row = i * sub
            for h in range(sub // half):
                top = lo_row + h * half
                for g in heads:
                    cum_b = cum[g][top:top + half, :]
                    q_b = q[g][top:top + half, :]
                    a = jnp.zeros((half, c), F32)
                    for s in range((h + 1) * half):
                        r = lo_row + s
                        diff = cum_b - key_ref[g, r:r + 1, :]
                        if s > h * half:
                            diff = jnp.where(trow >= s - h * half, diff, -jnp.inf)
                        p = q_b * jnp.exp2(diff)
                        a = jnp.where(lane == r, jnp.sum(p, axis=-1, keepdims=True), a)
                    if i > 0:
                        a = a + off[g, i][h * half:(h + 1) * half, :]
                    blocks[g].append(a)

        for g in heads:
            scores = jnp.concatenate(blocks[g], axis=0)
            og = o[g] + _dot(scores.astype(BF16), vb[g])
            r = lax.rsqrt(jnp.mean(og * og, axis=-1, keepdims=True) + RMS_EPS)
            o_ref[sl, cols[g]] = (((og * r) * gain) * gate_ref[sl, cols[g]]).astype(o_ref.dtype)
        return carry

    lax.fori_loop(0, rows_total // c, chunk, 0, unroll=HGRN_UNROLL)


def _hgrn(proj, norm_gain, w_cast):
    b, s, _ = proj.shape
    grp = HGRN_GROUP
    ng = N_HEADS // grp
    sb = min(HGRN_SEQ_BLOCK, s)
    assert s % sb == 0 and sb % (HGRN_CHUNK * HGRN_UNROLL) == 0
    grid = (b, ng, s // sb)
    col = lambda off: pl.BlockSpec((None, sb, grp * HEAD), lambda bi, gi, si: (bi, si, off + gi))
    cast_blk, n_cast = _hosted_cast_spec(
        w_cast, grid[0] * grid[1] * grid[2], lambda bi, gi, si: (bi * grid[1] + gi) * grid[2] + si)
    return pl.pallas_call(
        functools.partial(_hgrn_kernel, n_cast=n_cast),
        grid=grid,
        in_specs=[
            col(0), col(ng), col(2 * ng), col(3 * ng),
            pl.BlockSpec((1, HEAD), lambda bi, gi, si: (0, 0)),
            cast_blk,
        ],
        out_specs=[pl.BlockSpec((None, sb, grp * HEAD), lambda bi, gi, si: (bi, si, gi)), cast_blk],
        out_shape=[jax.ShapeDtypeStruct((b, s, MIX_WIDTH), BF16),
                   jax.ShapeDtypeStruct(w_cast.shape, BF16)],
        scratch_shapes=[
            pltpu.VMEM((grp, HEAD, HEAD), F32),
            pltpu.VMEM((grp, HGRN_CHUNK, HEAD), F32),
            pltpu.VMEM((grp, HGRN_CHUNK, HEAD), F32),
        ],
        compiler_params=pltpu.CompilerParams(
            dimension_semantics=("arbitrary", "arbitrary", "arbitrary"),
            vmem_limit_bytes=VMEM_LIMIT),
        name="hgrn2",
    )(proj, proj, proj, proj, norm_gain.reshape(1, HEAD), w_cast)


def _split3(v):
    hi = v.astype(BF16).astype(F32)
    mid = (v - hi).astype(BF16).astype(F32)
    lo = ((v - hi) - mid).astype(BF16).astype(F32)
    return hi, mid, lo


def _moba_kernel(slopes_ref, units_ref, q_ref, k_ref, v_ref, wg_ref, wu_ref, w_ref, o_ref, pack_ref,
                 wb_ref, kb_ref, vt_ref, kmean_ref, *, n_pack, n_cast):
    blk = MOBA_BLOCK
    seq = q_ref.shape[0]
    nb = seq // blk
    topk = min(MOBA_TOPK, max(nb - 1, 1))
    head = pl.program_id(1)
    to_log2 = (HEAD ** -0.5) * LOG2E
    slope2 = slopes_ref[head] * LOG2E
    unit = units_ref[head]
    neg_inf = jnp.full((1, blk), -jnp.inf, F32)

    k_pos = lax.broadcasted_iota(jnp.int32, (blk, HEAD), 0).astype(F32) * unit
    k_lane = lax.broadcasted_iota(jnp.int32, (blk, HEAD), 1)
    hi, mid, lo = _split3(k_pos)
    aug_k = jnp.where(k_lane == 0, hi, jnp.where(k_lane == 1, mid, jnp.where(
        k_lane == 2, lo, jnp.where(k_lane < 6, 1.0, 0.0)))).astype(BF16)
    q_pos = lax.broadcasted_iota(jnp.int32, (HEAD, blk), 1).astype(F32) * (-unit)
    q_row = lax.broadcasted_iota(jnp.int32, (HEAD, blk), 0)
    hi, mid, lo = _split3(q_pos)
    aug_q = jnp.where(q_row < 3, 1.0, jnp.where(q_row == 3, hi, jnp.where(
        q_row == 4, mid, jnp.where(q_row == 5, lo, 0.0)))).astype(BF16)

    @pl.when(_grid_step(2) < n_pack)
    def _():
        _pack_gate_up_kernel(wg_ref, wu_ref, pack_ref)

    @pl.when(_grid_step(2) < n_cast)
    def _():
        wb_ref[...] = w_ref[...].astype(BF16)

    kmean_ref[...] = jnp.zeros_like(kmean_ref)
    for n in range(nb):
        rows = slice(n * blk, (n + 1) * blk)
        kblk = k_ref[rows, :]
        kmean_ref[n:n + 1, :] = jnp.mean(kblk, axis=0, keepdims=True)
        kb_ref[rows, :HEAD] = kblk.astype(BF16)
        kb_ref[rows, HEAD:] = aug_k
        vt_ref[:, rows] = v_ref[rows, :].T.astype(BF16)

    causal = (lax.broadcasted_iota(jnp.int32, (blk, blk), 0)
              <= lax.broadcasted_iota(jnp.int32, (blk, blk), 1))

    def score(qb):
        qsl = slice(qb * blk, (qb + 1) * blk)
        qt = q_ref[qsl, :].T
        if qb <= topk:
            picked = [None] * qb
        else:
            rows_p = -(-qb // 8) * 8
            blk_row = lax.broadcasted_iota(jnp.int32, (rows_p, blk), 0)
            gate = _dot(kmean_ref[0:rows_p, :], qt, precision=lax.Precision.HIGHEST)
            gate = jnp.where(blk_row < qb, gate, -jnp.inf)
            sel = jnp.zeros((rows_p, blk), F32)
            for _ in range(topk):
                best = jnp.max(gate, axis=0, keepdims=True)
                first = jnp.min(jnp.where(gate == best, blk_row, rows_p), axis=0, keepdims=True)
                pick = blk_row == first
                sel = jnp.where(pick, 1.0, sel)
                gate = jnp.where(pick, -jnp.inf, gate)
            picked = [sel[n:n + 1, :] > 0.0 for n in range(qb)]

        nk = (qb + 1) * blk
        qt_aug = jnp.concatenate([qt.astype(BF16), aug_q], axis=0)
        raw = _dot(kb_ref[0:nk, :], qt_aug)
        return raw, picked

    def tile_max(qb, raw, picked):
        nk = (qb + 1) * blk
        tiles, shifts = [], []
        for n in range(qb):
            tiles.append(raw[n * blk:(n + 1) * blk, :] * to_log2)
            offset = slope2 * float((n - qb) * blk)
            shifts.append(offset if picked[n] is None else jnp.where(picked[n], offset, neg_inf))
        tiles.append(jnp.where(causal, raw[qb * blk:nk, :] * to_log2, -jnp.inf))
        shifts.append(0.0)
        m = None
        for t, sh in zip(tiles, shifts):
            top = jnp.max(t, axis=0, keepdims=True) + sh
            m = top if m is None else jnp.maximum(m, top)
        return tiles, [sh - m for sh in shifts]

    def softmax(qb, tiles, shifts):
        probs, denom = [], None
        for t, sh in zip(tiles, shifts):
            p = jnp.exp2(t + sh)
            probs.append(p.astype(BF16))
            part = jnp.sum(p, axis=0, keepdims=True)
            denom = part if denom is None else denom + part
        return jnp.concatenate(probs, axis=0), denom

    def output(qb, probs, denom):
        nk = (qb + 1) * blk
        acc = _dot(vt_ref[:, 0:nk], probs)
        o_ref[qb * blk:nk, :] = (acc / denom).T.astype(o_ref.dtype)

    stages = [score, tile_max, softmax, output]
    live = {}
    for step in range(nb + len(stages) - 1):
        for k, stage in enumerate(stages):
            qb = step - k
            if 0 <= qb < nb:
                args = live.pop((qb, k - 1)) if k else ()
                live[qb, k] = stage(qb, *args)


def _moba(proj, w_gate, w_up, w_cast):
    b, s, _ = proj.shape
    assert s % MOBA_BLOCK == 0
    nb = s // MOBA_BLOCK
    nbp = -(-nb // 8) * 8
    d, d_ff = w_gate.shape
    n_pack = d_ff // FF_TILE
    assert n_pack <= b * N_HEADS
    slopes = jnp.exp2(-8.0 * jnp.arange(1, N_HEADS + 1, dtype=F32) / N_HEADS)
    col = lambda off: pl.BlockSpec((None, s, HEAD), lambda bi, hi: (bi, 0, off + hi))
    tile_of = lambda bi, hi: jnp.minimum(bi * N_HEADS + hi, n_pack - 1)
    w_col = pl.BlockSpec((d, FF_TILE), lambda bi, hi: (0, tile_of(bi, hi)))
    cast_blk, n_cast = _hosted_cast_spec(w_cast, b * N_HEADS, lambda bi, hi: bi * N_HEADS + hi)
    return pl.pallas_call(
        functools.partial(_moba_kernel, n_pack=n_pack, n_cast=n_cast),
        grid=(b, N_HEADS),
        in_specs=[
            pl.BlockSpec(memory_space=pltpu.SMEM),
            pl.BlockSpec(memory_space=pltpu.SMEM),
            col(4 * N_HEADS), col(5 * N_HEADS), col(6 * N_HEADS),
            w_col, w_col, cast_blk,
        ],
        out_specs=[
            pl.BlockSpec((None, s, HEAD), lambda bi, hi: (bi, 0, hi)),
            pl.BlockSpec((None, d, 2 * FF_TILE), lambda bi, hi: (tile_of(bi, hi), 0, 0)),
            cast_blk,
        ],
        out_shape=[jax.ShapeDtypeStruct((b, s, MIX_WIDTH), BF16),
                   jax.ShapeDtypeStruct((n_pack, d, 2 * FF_TILE), BF16),
                   jax.ShapeDtypeStruct(w_cast.shape, BF16)],
        scratch_shapes=[
            pltpu.VMEM((s, 2 * HEAD), BF16),
            pltpu.VMEM((HEAD, s), BF16),
            pltpu.VMEM((nbp, HEAD), F32),
        ],
        compiler_params=pltpu.CompilerParams(
            dimension_semantics=("arbitrary", "arbitrary"),
            vmem_limit_bytes=VMEM_LIMIT),
        name="moba",
    )(slopes, slopes * HEAD ** 0.5, proj, proj, proj, w_gate, w_up, w_cast)


def kernel(x, ffn1_norm, ffn1_w_gate, ffn1_w_up, ffn1_w_down, mix_norm, w_in, hgrn_lower_bounds, hgrn_out_norm, w_out, ffn2_norm, ffn2_w_gate, ffn2_w_up, ffn2_w_down, final_norm):
    b, s, d = x.shape
    depth = ffn1_norm.shape[0]
    lb_all = jnp.cumsum(jax.nn.softmax(hgrn_lower_bounds.astype(F32), axis=0), axis=0)
    xs = x.reshape(b * s, d)
    for layer in range(depth):
        xs, w_in_b = _ffn(xs, ffn1_norm[layer], _prep_gate_up(ffn1_w_gate[layer], ffn1_w_up[layer]),
                          ffn1_w_down[layer].astype(BF16), w_cast=w_in[layer])
        proj = _in_proj(xs, mix_norm[layer], w_in_b, lb_all[layer])
        proj = proj.reshape(b, s, -1)
        o_hgrn, wd2 = _hgrn(proj, hgrn_out_norm[layer], ffn2_w_down[layer])
        o_moba, wgu2, w_out_b = _moba(proj, ffn2_w_gate[layer], ffn2_w_up[layer], w_out[layer])
        xs = _out_proj(o_hgrn.reshape(b * s, MIX_WIDTH), o_moba.reshape(b * s, MIX_WIDTH),
                       w_out_b, xs)
        last = layer == depth - 1
        xs, = _ffn(xs, ffn2_norm[layer], wgu2, wd2, final_norm if last else None)
    if depth == 0:
        raise NotImplementedError("depth 0 has no FFN to carry the final norm")
    return xs.reshape(b, s, d)
```

```python
import functools

import jax
import jax.numpy as jnp
from jax import lax
from jax.experimental import pallas as pl
from jax.experimental.pallas import tpu as pltpu

F32 = jnp.float32
BF16 = jnp.bfloat16

HEAD = 128
N_HEADS = 16
MIX_WIDTH = N_HEADS * HEAD
MOBA_BLOCK = 256
MOBA_TOPK = 3
RMS_EPS = 1e-6
MACARON_WEIGHT = 0.5
LOG2E = 1.4426950408889634

HGRN_CHUNK = 64
SUBLANES = 8
LANES = 128
HGRN_SUB = 8
HGRN_GROUP = 8
HGRN_SEQ_BLOCK = 1024
HGRN_UNROLL = 4
FF_TILE = 256
FF_PAIR = 2
NORM_ROWS = 16
NORM_UNROLL = 4

V7X_VMEM_BYTES = 64 * 1024 * 1024
VMEM_LIMIT = V7X_VMEM_BYTES - 6 * 1024 * 1024
FFN_VMEM_LIMIT = V7X_VMEM_BYTES - 1 * 1024 * 1024


def _dot(a, b, precision=None):
    return jnp.dot(a, b, preferred_element_type=F32, precision=precision)


def _dot_nt(a, b):
    return lax.dot_general(a, b, (((1,), (1,)), ((), ())), preferred_element_type=F32)


def _dot_tn(a, b):
    return lax.dot_general(a, b, (((0,), (0,)), ((), ())), preferred_element_type=F32)


def _grid_step(rank):
    step = pl.program_id(0)
    for axis in range(1, rank):
        step = step * pl.num_programs(axis) + pl.program_id(axis)
    return step


def _cast_blocks(rows, cols, steps):
    for n in range(min(steps, rows // NORM_ROWS), 0, -1):
        if rows % n == 0 and (rows // n) % NORM_ROWS == 0:
            split = 1
            while n * split * 2 <= steps and cols % (split * 2 * LANES) == 0:
                split *= 2
            return rows // n, cols // split
    raise ValueError(f"cannot split {rows} rows into bf16 row-tile blocks")


def _hosted_cast_spec(w, steps, step_of):
    rows, cols = w.shape
    blk_rows, blk_cols = _cast_blocks(rows, cols, steps)
    col_blocks = cols // blk_cols
    n_cast = (rows // blk_rows) * col_blocks

    def index(*grid_idx):
        blk = jnp.minimum(step_of(*grid_idx), n_cast - 1)
        return blk // col_blocks, blk % col_blocks

    return pl.BlockSpec((blk_rows, blk_cols), index), n_cast


def _rmsnorm_rows(x_ref, g_ref, o_ref, unroll=NORM_UNROLL):
    groups = x_ref.shape[0] // NORM_ROWS
    gain = g_ref[...]

    def rows(i):
        return pl.ds(pl.multiple_of(i * NORM_ROWS, NORM_ROWS), NORM_ROWS)

    def inv_rms(i):
        x = x_ref[rows(i), :]
        return lax.rsqrt(jnp.mean(x * x, axis=-1, keepdims=True) + RMS_EPS)

    def body(i, r):
        r_next = inv_rms(jnp.minimum(i + 1, groups - 1))
        o_ref[rows(i), :] = ((x_ref[rows(i), :] * r) * gain).astype(o_ref.dtype)
        return r_next

    lax.fori_loop(0, groups, body, inv_rms(0), unroll=unroll)


def _ffn_kernel(*refs, final_norm, n_tiles, n_cast):
    refs = list(refs)
    x_hbm, g_ref, wgu_ref, wd_ref = refs[:4]
    del refs[:4]
    gf_ref = refs.pop(0) if final_norm else None
    w_ref = refs.pop(0) if n_cast else None
    o_ref = refs.pop(0)
    wb_ref = refs.pop(0) if n_cast else None
    x_ref, h_ref, sem = refs
    i = pl.program_id(0)
    f = pl.program_id(1)
    last = pl.num_programs(1) - 1
    more_rows = i + 1 < pl.num_programs(0)
    tail = n_tiles - last * FF_PAIR
    slot = i % 2
    tm = x_ref.shape[0]

    def fetch(row_tile):
        return pltpu.make_async_copy(x_hbm.at[pl.ds(row_tile * tm, tm), :], x_ref, sem)

    if n_cast:
        @pl.when(_grid_step(2) < n_cast)
        def _():
            wb_ref[...] = w_ref[...].astype(BF16)

    def tile(k):
        gu = _dot(h_ref[slot], wgu_ref[k])
        gate = gu[:, :FF_TILE]
        up = gu[:, FF_TILE:]
        act = (gate * jax.nn.sigmoid(gate)) * up * MACARON_WEIGHT
        return _dot(act.astype(BF16), wd_ref[k * FF_TILE:(k + 1) * FF_TILE, :])

    @pl.when(f == 0)
    def _():
        @pl.when(i == 0)
        def _():
            fetch(0).start()
            fetch(0).wait()
            _rmsnorm_rows(x_ref, g_ref, h_ref.at[0])

        o_ref[...] = x_ref[...] + tile(0)
        for k in range(1, FF_PAIR):
            o_ref[...] += tile(k)

    @pl.when(jnp.logical_and(f == 1, more_rows))
    def _():
        fetch(i + 1).start()

    @pl.when(jnp.logical_and(f > 0, f < last))
    def _():
        for k in range(FF_PAIR):
            o_ref[...] += tile(k)

    @pl.when(f == last)
    def _():
        @pl.when(more_rows)
        def _():
            fetch(i + 1).wait()

        _rmsnorm_rows(x_ref, g_ref, h_ref.at[1 - slot], unroll=True)
        for k in range(tail):
            o_ref[...] += tile(k)
        if final_norm:
            _rmsnorm_rows(o_ref, gf_ref, o_ref)


def _ffn(x, gain, wgu, wd, final_gain=None, w_cast=None, *, tm=512):
    t, d = x.shape
    nf = wgu.shape[0]
    steps = pl.cdiv(nf, FF_PAIR)
    assert steps >= 2 and t % tm == 0
    final_norm = final_gain is not None
    in_specs = [
        pl.BlockSpec(memory_space=pl.ANY),
        pl.BlockSpec((1, d), lambda i, f: (0, 0)),
        pl.BlockSpec((FF_PAIR, d, 2 * FF_TILE), lambda i, f: (f, 0, 0)),
        pl.BlockSpec((FF_PAIR * FF_TILE, d), lambda i, f: (f, 0)),
    ]
    args = [x, gain.reshape(1, d), wgu, wd]
    if final_norm:
        in_specs.append(pl.BlockSpec((1, d), lambda i, f: (0, 0)))
        args.append(final_gain.reshape(1, d))
    out_specs = [pl.BlockSpec((tm, d), lambda i, f: (i, 0))]
    out_shape = [jax.ShapeDtypeStruct((t, d), F32)]
    n_cast = 0
    if w_cast is not None:
        cast_blk, n_cast = _hosted_cast_spec(w_cast, (t // tm) * steps, lambda i, f: i * steps + f)
        in_specs.append(cast_blk)
        args.append(w_cast)
        out_specs.append(cast_blk)
        out_shape.append(jax.ShapeDtypeStruct(w_cast.shape, BF16))
    return pl.pallas_call(
        functools.partial(_ffn_kernel, final_norm=final_norm, n_tiles=nf, n_cast=n_cast),
        grid=(t // tm, steps),
        in_specs=in_specs,
        out_specs=out_specs,
        out_shape=out_shape,
        scratch_shapes=[
            pltpu.VMEM((tm, d), F32),
            pltpu.VMEM((2, tm, d), BF16),
            pltpu.SemaphoreType.DMA(()),
        ],
        compiler_params=pltpu.CompilerParams(
            dimension_semantics=("arbitrary", "arbitrary"),
            vmem_limit_bytes=FFN_VMEM_LIMIT),
        name="ffn_final" if final_norm else "ffn",
    )(*args)


def _pack_gate_up_kernel(wg_ref, wu_ref, o_ref):
    o_ref[:, :FF_TILE] = wg_ref[...].astype(BF16)
    o_ref[:, FF_TILE:] = wu_ref[...].astype(BF16)


def _prep_gate_up(w_gate, w_up):
    d, d_ff = w_gate.shape
    nf = d_ff // FF_TILE
    col = pl.BlockSpec((d, FF_TILE), lambda f: (0, f))
    return pl.pallas_call(
        _pack_gate_up_kernel,
        grid=(nf,),
        in_specs=[col, col],
        out_specs=pl.BlockSpec((None, d, 2 * FF_TILE), lambda f: (f, 0, 0)),
        out_shape=jax.ShapeDtypeStruct((nf, d, 2 * FF_TILE), BF16),
        compiler_params=pltpu.CompilerParams(
            dimension_semantics=("parallel",), vmem_limit_bytes=VMEM_LIMIT),
        name="pack_gate_up",
    )(w_gate, w_up)


def _in_proj_kernel(x_hbm, g_ref, w_ref, lb_ref, o_ref, x_ref, h_ref, sem, *, tiles_per_part):
    i = pl.program_id(0)
    j = pl.program_id(1)
    part = j // tiles_per_part
    tm = x_ref.shape[0]

    def fetch(tile):
        return pltpu.make_async_copy(x_hbm.at[pl.ds(tile * tm, tm), :], x_ref, sem)

    @pl.when(j == 0)
    def _():
        @pl.when(i == 0)
        def _():
            fetch(0).start()

        fetch(i).wait()
        _rmsnorm_rows(x_ref, g_ref, h_ref)

    @pl.when(jnp.logical_and(j == 1, i + 1 < pl.num_programs(0)))
    def _():
        fetch(i + 1).start()

    @pl.when(jnp.logical_or(part == 0, part == 3))
    def _():
        z = _dot(h_ref[...], w_ref[...])
        o_ref[...] = (z * jax.nn.sigmoid(z)) * jnp.where(part == 0, HEAD ** -0.5, 1.0)

    @pl.when(part == 1)
    def _():
        lb = lb_ref[...]
        o_ref[...] = lb + (1.0 - lb) * jax.nn.sigmoid(_dot(h_ref[...], w_ref[...]))

    @pl.when(jnp.logical_or(part == 2, part > 3))
    def _():
        o_ref[...] = _dot(h_ref[...], w_ref[...])


def _in_proj(x, gain, w, lower_bound, *, tm=1024, tn=1024):
    t, d = x.shape
    n = w.shape[1]
    tiles_per_part = MIX_WIDTH // tn
    assert n // tn >= 2 and t % tm == 0
    return pl.pallas_call(
        functools.partial(_in_proj_kernel, tiles_per_part=tiles_per_part),
        grid=(t // tm, n // tn),
        in_specs=[
            pl.BlockSpec(memory_space=pl.ANY),
            pl.BlockSpec((1, d), lambda i, j: (0, 0)),
            pl.BlockSpec((d, tn), lambda i, j: (0, j)),
            pl.BlockSpec((1, tn),
                         lambda i, j: (0, jnp.clip(j - tiles_per_part, 0, tiles_per_part - 1))),
        ],
        out_specs=pl.BlockSpec((tm, tn), lambda i, j: (i, j)),
        out_shape=jax.ShapeDtypeStruct((t, n), F32),
        scratch_shapes=[
            pltpu.VMEM((tm, d), F32),
            pltpu.VMEM((tm, d), BF16),
            pltpu.SemaphoreType.DMA(()),
        ],
        compiler_params=pltpu.CompilerParams(
            dimension_semantics=("arbitrary", "arbitrary"),
            vmem_limit_bytes=VMEM_LIMIT),
        name="in_proj",
    )(x, gain.reshape(1, d), w, lower_bound.reshape(1, MIX_WIDTH))


def _out_proj_kernel(oh_ref, om_ref, wh_ref, wm_ref, x_ref, o_ref):
    o_ref[...] = x_ref[...] + _dot(oh_ref[...], wh_ref[...]) + _dot(om_ref[...], wm_ref[...])


def _out_proj(o_hgrn, o_moba, w, x, *, tm=1024, tn=1024):
    t, d = x.shape
    kh = o_hgrn.shape[1]
    return pl.pallas_call(
        _out_proj_kernel,
        grid=(t // tm, d // tn),
        in_specs=[
            pl.BlockSpec((tm, kh), lambda i, j: (i, 0)),
            pl.BlockSpec((tm, kh), lambda i, j: (i, 0)),
            pl.BlockSpec((kh, tn), lambda i, j: (0, j)),
            pl.BlockSpec((kh, tn), lambda i, j: (1, j)),
            pl.BlockSpec((tm, tn), lambda i, j: (i, j)),
        ],
        out_specs=pl.BlockSpec((tm, tn), lambda i, j: (i, j)),
        out_shape=jax.ShapeDtypeStruct((t, d), F32),
        compiler_params=pltpu.CompilerParams(
            dimension_semantics=("parallel", "arbitrary"),
            vmem_limit_bytes=VMEM_LIMIT),
        name="out_proj",
    )(o_hgrn, o_moba, w, w, x)


def _hgrn_kernel(q_ref, f_ref, v_ref, gate_ref, gain_ref, w_ref, o_ref, wb_ref,
                 st_ref, cum_ref, key_ref, *, n_cast):
    c, sub, half = HGRN_CHUNK, HGRN_SUB, SUBLANES
    rows_total = q_ref.shape[0]
    gain = gain_ref[...]
    tri = (lax.broadcasted_iota(jnp.int32, (c, c), 0)
           >= lax.broadcasted_iota(jnp.int32, (c, c), 1)).astype(BF16)
    lane = lax.broadcasted_iota(jnp.int32, (half, c), 1)
    trow = lax.broadcasted_iota(jnp.int32, (half, HEAD), 0)

    @pl.when(_grid_step(3) < n_cast)
    def _():
        wb_ref[...] = w_ref[...].astype(BF16)

    @pl.when(pl.program_id(2) == 0)
    def _():
        st_ref[...] = jnp.zeros_like(st_ref)

    def chunk(ci, carry):
        sl = pl.ds(pl.multiple_of(ci * c, c), c)
        forget_all = f_ref[sl, :]
        logf = jnp.log2(forget_all)
        hi = logf.astype(BF16)
        rest = logf - hi.astype(F32)
        mid = rest.astype(BF16)
        lo = (rest - mid.astype(F32)).astype(BF16)
        cum_all = _dot(tri, hi) + _dot(tri, mid) + _dot(tri, lo)
        key_all = cum_all - jnp.log2(1.0 - forget_all)

        heads = range(HGRN_GROUP)
        cols = [slice(g * HEAD, (g + 1) * HEAD) for g in heads]
        cum = [cum_all[:, cols[g]] for g in heads]
        key = [key_all[:, cols[g]] for g in heads]
        q, vb, o = [], [], []
        for g in heads:
            q.append(q_ref[sl, cols[g]])
            vb.append(v_ref[sl, cols[g]].astype(BF16))
            cum_ref[g] = cum[g]
            key_ref[g] = key[g]
        for g in heads:
            last = cum[g][c - 1:c, :]
            st = st_ref[g]
            o.append(_dot_nt((q[g] * jnp.exp2(cum[g])).astype(BF16), st.astype(BF16)))
            k_end = jnp.exp2(last - key[g]).astype(BF16)
            st_ref[g] = st * jnp.exp2(last) + _dot_tn(vb[g], k_end)

        off = {}
        for i in range(1, c // sub):
            lo_row = i * sub
            for g in heads:
                edge = cum_ref[g, lo_row - 1:lo_row, :]
                q_t = (q[g][lo_row:lo_row + sub, :]
                       * jnp.exp2(cum[g][lo_row:lo_row + sub, :] - edge)).astype(BF16)
                k_t = jnp.exp2(edge - key[g][:lo_row, :])
                k_t = jnp.concatenate([k_t, jnp.zeros((c - lo_row, HEAD), F32)], axis=0)
                off[g, i] = _dot_nt(q_t, k_t.astype(BF16))

        blocks = [[] for _ in heads]
        for i in range(c // sub):
            lo_row = i * sub
            for h in range(sub // half):
                top = lo_row + h * half
                for g in heads:
                    cum_b = cum[g][top:top + half, :]
                    q_b = q[g][top:top + half, :]
                    a = jnp.zeros((half, c), F32)
                    for s in range((h + 1) * half):
                        r = lo_row + s
                        diff = cum_b - key_ref[g, r:r + 1, :]
                        if s > h * half:
                            diff = jnp.where(trow >= s - h * half, diff, -jnp.inf)
                        p = q_b * jnp.exp2(diff)
                        a = jnp.where(lane == r, jnp.sum(p, axis=-1, keepdims=True), a)
                    if i > 0:
                        a = a + off[g, i][h * half:(h + 1) * half, :]
                    blocks[g].append(a)

        for g in heads:
            scores = jnp.concatenate(blocks[g], axis=0)
            og = o[g] + _dot(scores.astype(BF16), vb[g])
            r = lax.rsqrt(jnp.mean(og * og, axis=-1, keepdims=True) + RMS_EPS)
            o_ref[sl, cols[g]] = (((og * r) * gain) * gate_ref[sl, cols[g]]).astype(o_ref.dtype)
        return carry

    lax.fori_loop(0, rows_total // c, chunk, 0, unroll=HGRN_UNROLL)


def _hgrn(proj, norm_gain, w_cast):
    b, s, _ = proj.shape
    grp = HGRN_GROUP
    ng = N_HEADS // grp
    sb = min(HGRN_SEQ_BLOCK, s)
    assert s % sb == 0 and sb % (HGRN_CHUNK * HGRN_UNROLL) == 0
    grid = (b, ng, s // sb)
    col = lambda off: pl.BlockSpec((None, sb, grp * HEAD), lambda bi, gi, si: (bi, si, off + gi))
    cast_blk, n_cast = _hosted_cast_spec(
        w_cast, grid[0] * grid[1] * grid[2], lambda bi, gi, si: (bi * grid[1] + gi) * grid[2] + si)
    return pl.pallas_call(
        functools.partial(_hgrn_kernel, n_cast=n_cast),
        grid=grid,
        in_specs=[
            col(0), col(ng), col(2 * ng), col(3 * ng),
            pl.BlockSpec((1, HEAD), lambda bi, gi, si: (0, 0)),
            cast_blk,
        ],
        out_specs=[pl.BlockSpec((None, sb, grp * HEAD), lambda bi, gi, si: (bi, si, gi)), cast_blk],
        out_shape=[jax.ShapeDtypeStruct((b, s, MIX_WIDTH), BF16),
                   jax.ShapeDtypeStruct(w_cast.shape, BF16)],
        scratch_shapes=[
            pltpu.VMEM((grp, HEAD, HEAD), F32),
            pltpu.VMEM((grp, HGRN_CHUNK, HEAD), F32),
            pltpu.VMEM((grp, HGRN_CHUNK, HEAD), F32),
        ],
        compiler_params=pltpu.CompilerParams(
            dimension_semantics=("arbitrary", "arbitrary", "arbitrary"),
            vmem_limit_bytes=VMEM_LIMIT),
        name="hgrn2",
    )(proj, proj, proj, proj, norm_gain.reshape(1, HEAD), w_cast)


def _split3(v):
    hi = v.astype(BF16).astype(F32)
    mid = (v - hi).astype(BF16).astype(F32)
    lo = ((v - hi) - mid).astype(BF16).astype(F32)
    return hi, mid, lo


def _moba_kernel(slopes_ref, units_ref, q_ref, k_ref, v_ref, wg_ref, wu_ref, w_ref, o_ref, pack_ref,
                 wb_ref, kb_ref, vt_ref, kmean_ref, *, n_pack, n_cast):
    blk = MOBA_BLOCK
    seq = q_ref.shape[0]
    nb = seq // blk
    topk = min(MOBA_TOPK, max(nb - 1, 1))
    head = pl.program_id(1)
    to_log2 = (HEAD ** -0.5) * LOG2E
    slope2 = slopes_ref[head] * LOG2E
    unit = units_ref[head]
    neg_inf = jnp.full((1, blk), -jnp.inf, F32)

    k_pos = lax.broadcasted_iota(jnp.int32, (blk, HEAD), 0).astype(F32) * unit
    k_lane = lax.broadcasted_iota(jnp.int32, (blk, HEAD), 1)
    hi, mid, lo = _split3(k_pos)
    aug_k = jnp.where(k_lane == 0, hi, jnp.where(k_lane == 1, mid, jnp.where(
        k_lane == 2, lo, jnp.where(k_lane < 6, 1.0, 0.0)))).astype(BF16)
    q_pos = lax.broadcasted_iota(jnp.int32, (HEAD, blk), 1).astype(F32) * (-unit)
    q_row = lax.broadcasted_iota(jnp.int32, (HEAD, blk), 0)
    hi, mid, lo = _split3(q_pos)
    aug_q = jnp.where(q_row < 3, 1.0, jnp.where(q_row == 3, hi, jnp.where(
        q_row == 4, mid, jnp.where(q_row == 5, lo, 0.0)))).astype(BF16)

    @pl.when(_grid_step(2) < n_pack)
    def _():
        _pack_gate_up_kernel(wg_ref, wu_ref, pack_ref)

    @pl.when(_grid_step(2) < n_cast)
    def _():
        wb_ref[...] = w_ref[...].astype(BF16)

    kmean_ref[...] = jnp.zeros_like(kmean_ref)
    for n in range(nb):
        rows = slice(n * blk, (n + 1) * blk)
        kblk = k_ref[rows, :]
        kmean_ref[n:n + 1, :] = jnp.mean(kblk, axis=0, keepdims=True)
        kb_ref[rows, :HEAD] = kblk.astype(BF16)
        kb_ref[rows, HEAD:] = aug_k
        vt_ref[:, rows] = v_ref[rows, :].T.astype(BF16)

    causal = (lax.broadcasted_iota(jnp.int32, (blk, blk), 0)
              <= lax.broadcasted_iota(jnp.int32, (blk, blk), 1))

    def score(qb):
        qsl = slice(qb * blk, (qb + 1) * blk)
        qt = q_ref[qsl, :].T
        if qb <= topk:
            picked = [None] * qb
        else:
            rows_p = -(-qb // 8) * 8
            blk_row = lax.broadcasted_iota(jnp.int32, (rows_p, blk), 0)
            gate = _dot(kmean_ref[0:rows_p, :], qt, precision=lax.Precision.HIGHEST)
            gate = jnp.where(blk_row < qb, gate, -jnp.inf)
            sel = jnp.zeros((rows_p, blk), F32)
            for _ in range(topk):
                best = jnp.max(gate, axis=0, keepdims=True)
                first = jnp.min(jnp.where(gate == best, blk_row, rows_p), axis=0, keepdims=True)
                pick = blk_row == first
                sel = jnp.where(pick, 1.0, sel)
                gate = jnp.where(pick, -jnp.inf, gate)
            picked = [sel[n:n + 1, :] > 0.0 for n in range(qb)]

        nk = (qb + 1) * blk
        qt_aug = jnp.concatenate([qt.astype(BF16), aug_q], axis=0)
        raw = _dot(kb_ref[0:nk, :], qt_aug)
        return raw, picked

    def tile_max(qb, raw, picked):
        nk = (qb + 1) * blk
        tiles, shifts = [], []
        for n in range(qb):
            tiles.append(raw[n * blk:(n + 1) * blk, :] * to_log2)
            offset = slope2 * float((n - qb) * blk)
            shifts.append(offset if picked[n] is None else jnp.where(picked[n], offset, neg_inf))
        tiles.append(jnp.where(causal, raw[qb * blk:nk, :] * to_log2, -jnp.inf))
        shifts.append(0.0)
        m = None
        for t, sh in zip(tiles, shifts):
            top = jnp.max(t, axis=0, keepdims=True) + sh
            m = top if m is None else jnp.maximum(m, top)
        return tiles, [sh - m for sh in shifts]

    def softmax(qb, tiles, shifts):
        probs, denom = [], None
        for t, sh in zip(tiles, shifts):
            p = jnp.exp2(t + sh)
            probs.append(p.astype(BF16))
            part = jnp.sum(p, axis=0, keepdims=True)
            denom = part if denom is None else denom + part
        return jnp.concatenate(probs, axis=0), denom

    def output(qb, probs, denom):
        nk = (qb + 1) * blk
        acc = _dot(vt_ref[:, 0:nk], probs)
        o_ref[qb * blk:nk, :] = (acc / denom).T.astype(o_ref.dtype)

    stages = [score, tile_max, softmax, output]
    live = {}
    for step in range(nb + len(stages) - 1):
        for k, stage in enumerate(stages):
            qb = step - k
            if 0 <= qb < nb:
                args = live.pop((qb, k - 1)) if k else ()
                live[qb, k] = stage(qb, *args)


def _moba(proj, w_gate, w_up, w_cast):
    b, s, _ = proj.shape
    assert s % MOBA_BLOCK == 0
    nb = s // MOBA_BLOCK
    nbp = -(-nb // 8) * 8
    d, d_ff = w_gate.shape
    n_pack = d_ff // FF_TILE
    assert n_pack <= b * N_HEADS
    slopes = jnp.exp2(-8.0 * jnp.arange(1, N_HEADS + 1, dtype=F32) / N_HEADS)
    col = lambda off: pl.BlockSpec((None, s, HEAD), lambda bi, hi: (bi, 0, off + hi))
    tile_of = lambda bi, hi: jnp.minimum(bi * N_HEADS + hi, n_pack - 1)
    w_col = pl.BlockSpec((d, FF_TILE), lambda bi, hi: (0, tile_of(bi, hi)))
    cast_blk, n_cast = _hosted_cast_spec(w_cast, b * N_HEADS, lambda bi, hi: bi * N_HEADS + hi)
    return pl.pallas_call(
        functools.partial(_moba_kernel, n_pack=n_pack, n_cast=n_cast),
        grid=(b, N_HEADS),
        in_specs=[
            pl.BlockSpec(memory_space=pltpu.SMEM),
            pl.BlockSpec(memory_space=pltpu.SMEM),
            col(4 * N_HEADS), col(5 * N_HEADS), col(6 * N_HEADS),
            w_col, w_col, cast_blk,
        ],
        out_specs=[
            pl.BlockSpec((None, s, HEAD), lambda bi, hi: (bi, 0, hi)),
            pl.BlockSpec((None, d, 2 * FF_TILE), lambda bi, hi: (tile_of(bi, hi), 0, 0)),
            cast_blk,
        ],
        out_shape=[jax.ShapeDtypeStruct((b, s, MIX_WIDTH), BF16),
                   jax.ShapeDtypeStruct((n_pack, d, 2 * FF_TILE), BF16),
                   jax.ShapeDtypeStruct(w_cast.shape, BF16)],
        scratch_shapes=[
            pltpu.VMEM((s, 2 * HEAD), BF16),
            pltpu.VMEM((HEAD, s), BF16),
            pltpu.VMEM((nbp, HEAD), F32),
        ],
        compiler_params=pltpu.CompilerParams(
            dimension_semantics=("arbitrary", "arbitrary"),
            vmem_limit_bytes=VMEM_LIMIT),
        name="moba",
    )(slopes, slopes * HEAD ** 0.5, proj, proj, proj, w_gate, w_up, w_cast)


def kernel(x, ffn1_norm, ffn1_w_gate, ffn1_w_up, ffn1_w_down, mix_norm, w_in, hgrn_lower_bounds, hgrn_out_norm, w_out, ffn2_norm, ffn2_w_gate, ffn2_w_up, ffn2_w_down, final_norm):
    b, s, d = x.shape
    depth = ffn1_norm.shape[0]
    lb_all = jnp.cumsum(jax.nn.softmax(hgrn_lower_bounds.astype(F32), axis=0), axis=0)
    xs = x.reshape(b * s, d)
    for layer in range(depth):
        xs, w_in_b = _ffn(xs, ffn1_norm[layer], _prep_gate_up(ffn1_w_gate[layer], ffn1_w_up[layer]),
                          ffn1_w_down[layer].astype(BF16), w_cast=w_in[layer])
        proj = _in_proj(xs, mix_norm[layer], w_in_b, lb_all[layer])
        proj = proj.reshape(b, s, -1)
        o_hgrn, wd2 = _hgrn(proj, hgrn_out_norm[layer], ffn2_w_down[layer])
        o_moba, wgu2, w_out_b = _moba(proj, ffn2_w_gate[layer], ffn2_w_up[layer], w_out[layer])
        xs = _out_proj(o_hgrn.reshape(b * s, MIX_WIDTH), o_moba.reshape(b * s, MIX_WIDTH),
                       w_out_b, xs)
        last = layer == depth - 1
        xs, = _ffn(xs, ffn2_norm[layer], wgu2, wd2, final_norm if last else None)
    if depth == 0:
        raise NotImplementedError("depth 0 has no FFN to carry the final norm")
    return xs.reshape(b, s, d)
```
